```python
import math
import jax, jax.numpy as jnp
from jax import lax
import numpy as np

D_MODEL = 1024
BATCH = 16
SEQ = 256
DEPTH = 4
DEC_BATCH = 2
DEC_SEQ = 2048
PAST_LEN = 256

GRID_W = 64
N_HEADS = 8
N_KV_HEADS = 2
HEAD_DIM = 64
ATTN_WIDTH = N_HEADS * HEAD_DIM
KV_WIDTH = N_KV_HEADS * HEAD_DIM
SSM_WIDTH = D_MODEL - ATTN_WIDTH
SSM_GROUP_CH = 16
SSM_GROUPS = SSM_WIDTH // SSM_GROUP_CH
SSM_STATE = 64
IN_WIDTH = ATTN_WIDTH + 2 * KV_WIDTH + SSM_WIDTH
Q_BLOCK = 128
ROPE_THETA = 10000.0
ROPE_PAIRS = HEAD_DIM // 4
D_FF = 2752
N_EXPERTS = 8
TOP_K = 2
EXPERT_FF = 3584
N_DENSE = (DEPTH + 1) // 2
N_MOE = DEPTH // 2
DEEPNORM_ALPHA = (2 * DEPTH) ** 0.25
DEEPNORM_BETA = (8 * DEPTH) ** -0.25
NORM_EPS = 1e-6

kernel_name = "hymba_s5_gqa_flow_backbone_step"


def _layer_norm(x, g=None, b=None):
    xf = x.astype(jnp.float32)
    mu = jnp.mean(xf, axis=-1, keepdims=True)
    xc = xf - mu
    var = jnp.mean(jnp.square(xc), axis=-1, keepdims=True)
    y = xc * lax.rsqrt(var + NORM_EPS)
    if g is not None:
        y = y * g.astype(jnp.float32) + b.astype(jnp.float32)
    return y.astype(x.dtype)


def _rms_norm_heads(x, g):
    xf = x.astype(jnp.float32)
    y = xf * lax.rsqrt(jnp.mean(jnp.square(xf), axis=-1, keepdims=True) + NORM_EPS) * g.astype(jnp.float32)
    return y.astype(x.dtype)


def _axial_rope_tables(n_tokens):
    rows = n_tokens // GRID_W
    row = jnp.repeat(jnp.arange(rows), GRID_W).astype(jnp.float32)
    col = jnp.tile(jnp.arange(GRID_W), rows).astype(jnp.float32)
    freqs = ROPE_THETA ** (-jnp.arange(ROPE_PAIRS, dtype=jnp.float32) / ROPE_PAIRS)
    ang = jnp.stack([row[:, None] * freqs, col[:, None] * freqs], axis=1)
    return jnp.cos(ang), jnp.sin(ang)


def _apply_rope(x, cos, sin):
    bsz, n, h, _ = x.shape
    xr = x.astype(jnp.float32).reshape(bsz, n, h, 2, 2, ROPE_PAIRS)
    x1, x2 = xr[..., 0, :], xr[..., 1, :]
    cb, sb = cos[None, :, None], sin[None, :, None]
    out = jnp.stack([x1 * cb - x2 * sb, x2 * cb + x1 * sb], axis=-2)
    return out.reshape(x.shape).astype(x.dtype)


def _attend(q, k, v):
    bsz, n = q.shape[0], q.shape[1]
    nb = n // Q_BLOCK
    grp = N_HEADS // N_KV_HEADS
    qb = q.reshape(bsz, nb, Q_BLOCK, N_KV_HEADS, grp, HEAD_DIM).transpose(1, 0, 2, 3, 4, 5)
    scale = HEAD_DIM ** -0.5

    def block(qi):
        s = jnp.einsum('bqkgd,bmkd->bkgqm', qi, k).astype(jnp.float32) * scale
        p = jax.nn.softmax(s, axis=-1).astype(v.dtype)
        return jnp.einsum('bkgqm,bmkd->bqkgd', p, v)

    o = lax.map(block, qb)
    return o.transpose(1, 0, 2, 3, 4, 5).reshape(bsz, n, ATTN_WIDTH)


def _scan_op(e_i, e_j):
    a_i, b_i = e_i
    a_j, b_j = e_j
    return a_j * a_i, a_j * b_i + b_j


def _s5_bidirectional(u, lam_re, lam_im, log_step, b_re, b_im, c_re, c_im, d_skip, h0_re, h0_im):
    bsz, n, _ = u.shape
    uf = u.astype(jnp.float32).reshape(bsz, n, SSM_GROUPS, SSM_GROUP_CH)
    uc = uf.astype(jnp.complex64)
    y = jnp.zeros_like(uf)
    finals = []
    for direction in range(2):
        lam = lax.complex(lam_re[direction].astype(jnp.float32), lam_im[direction].astype(jnp.float32))
        delta = jnp.exp(log_step[direction].astype(jnp.float32))[:, None]
        lam_bar = jnp.exp(lam * delta)
        b = lax.complex(b_re[direction].astype(jnp.float32), b_im[direction].astype(jnp.float32))
        b_bar = ((lam_bar - 1.0) / lam)[..., None] * b
        bu = jnp.einsum('gph,bngh->bngp', b_bar, uc)
        if direction == 1:
            bu = bu[:, ::-1]
        h0 = lax.complex(h0_re[:, direction].astype(jnp.float32), h0_im[:, direction].astype(jnp.float32))
        bu = bu.at[:, 0].add(lam_bar * h0)
        a = jnp.broadcast_to(lam_bar, bu.shape)
        _, hs = lax.associative_scan(_scan_op, (a, bu), axis=1)
        finals.append(hs[:, -1])
        if direction == 1:
            hs = hs[:, ::-1]
        cm = lax.complex(c_re[direction].astype(jnp.float32), c_im[direction].astype(jnp.float32))
        y = y + jnp.real(jnp.einsum('ghp,bngp->bngh', cm, hs))
    y = y + d_skip.astype(jnp.float32).reshape(SSM_GROUPS, SSM_GROUP_CH) * uf
    final = jnp.stack(finals, axis=1)
    return y.reshape(bsz, n, SSM_WIDTH), jnp.real(final), jnp.imag(final)


def _mixer(h, p, ctx_k, ctx_v, h0_re, h0_im, rope):
    bsz, n, _ = h.shape
    proj = h @ p['w_in']
    q, k, v, u = jnp.split(proj, [ATTN_WIDTH, ATTN_WIDTH + KV_WIDTH, ATTN_WIDTH + 2 * KV_WIDTH], axis=-1)
    q = _rms_norm_heads(q.reshape(bsz, n, N_HEADS, HEAD_DIM), p['q_gain'])
    k = _rms_norm_heads(k.reshape(bsz, n, N_KV_HEADS, HEAD_DIM), p['k_gain'])
    v = v.reshape(bsz, n, N_KV_HEADS, HEAD_DIM)
    if rope is None:
        attn = _attend(q, k, v)
    else:
        cos, sin = rope
        k_all = jnp.concatenate([ctx_k.astype(k.dtype), _apply_rope(k, cos, sin)], axis=1)
        v_all = jnp.concatenate([ctx_v.astype(v.dtype), v], axis=1)
        attn = _attend(_apply_rope(q, cos, sin), k_all, v_all)
    y_ssm, fin_re, fin_im = _s5_bidirectional(u, p['lam_re'], p['lam_im'], p['log_step'], p['b_re'], p['b_im'],
                                              p['c_re'], p['c_im'], p['d_skip'], h0_re, h0_im)
    y_ssm = jax.nn.gelu(y_ssm)
    y_ssm = y_ssm * jax.nn.sigmoid(y_ssm @ p['w_glu'].astype(jnp.float32) + p['b_glu'].astype(jnp.float32))
    out = jnp.concatenate([attn, y_ssm.astype(h.dtype)], axis=-1) @ p['w_out']
    return out, k, v, fin_re, fin_im


def _swiglu(h, wg, wu, wd):
    return (jax.nn.silu(h @ wg) * (h @ wu)) @ wd


def _moe(h, router, wg, wu, wd):
    bsz, n, dm = h.shape
    t = h.reshape(bsz * n, dm)
    logits = (t @ router).astype(jnp.float32)
    top_v, top_i = lax.top_k(logits, TOP_K)
    top_w = jax.nn.softmax(top_v, axis=-1)
    gates = jnp.sum(jax.nn.one_hot(top_i, N_EXPERTS, dtype=jnp.float32) * top_w[..., None], axis=1)
    out = jnp.zeros((bsz * n, dm), jnp.float32)
    for e in range(N_EXPERTS):
        out = out + gates[:, e:e + 1] * _swiglu(t, wg[e], wu[e], wd[e]).astype(jnp.float32)
    return out.astype(h.dtype).reshape(bsz, n, dm)


def _layer(x, mod, p, ctx_k, ctx_v, h0_re, h0_im, rope):
    sh1, sc1, g1, sh2, sc2, g2 = jnp.split(mod, 6, axis=-1)
    h = _layer_norm(x) * (1 + sc1) + sh1
    mix, k, v, fin_re, fin_im = _mixer(h, p, ctx_k, ctx_v, h0_re, h0_im, rope)
    x = _layer_norm(DEEPNORM_ALPHA * x + g1 * mix, p['ln1_g'], p['ln1_b'])
    h = _layer_norm(x) * (1 + sc2) + sh2
    if 'router' in p:
        f = _moe(h, p['router'], p['moe_gate'], p['moe_up'], p['moe_down'])
    else:
        f = _swiglu(h, p['ffn_gate'], p['ffn_up'], p['ffn_down'])
    x = _layer_norm(DEEPNORM_ALPHA * x + g2 * f, p['ln2_g'], p['ln2_b'])
    return x, k, v, fin_re, fin_im


def setup_inputs(seed: int = 0) -> dict:
    key = jax.random.key(seed)
    ks = iter(jax.random.split(key, 40))
    f32 = jnp.float32

    def nrm(shape, scale=1.0):
        return jax.random.normal(next(ks), shape, f32) * scale

    G, P, H = SSM_GROUPS, SSM_STATE, SSM_GROUP_CH
    lam_im = jnp.broadcast_to(math.pi * jnp.arange(P, dtype=f32), (DEPTH, 2, G, P))
    log_step = jax.random.uniform(next(ks), (DEPTH, 2, G), f32, math.log(1e-3), math.log(1e-1))
    return {
        'x_prompt': nrm((BATCH, SEQ, D_MODEL)),
        'x_sample': nrm((DEC_BATCH, DEC_SEQ, D_MODEL)),
        'c': nrm((DEC_BATCH, D_MODEL)),
        'cache_k': nrm((DEC_BATCH, DEPTH, PAST_LEN, N_KV_HEADS, HEAD_DIM)),
        'cache_v': nrm((DEC_BATCH, DEPTH, PAST_LEN, N_KV_HEADS, HEAD_DIM)),
        'state_ssm_re': nrm((DEC_BATCH, DEPTH, 2, G, P), 0.5),
        'state_ssm_im': nrm((DEC_BATCH, DEPTH, 2, G, P), 0.5),
        'c_ctx': nrm((D_MODEL,)),
        'w_ada': nrm((DEPTH, D_MODEL, 6 * D_MODEL), 0.5 * D_MODEL ** -0.5),
        'b_ada': nrm((DEPTH, 6 * D_MODEL), 0.02),
        'w_in': nrm((DEPTH, D_MODEL, IN_WIDTH), D_MODEL ** -0.5),
        'q_gain': 1.0 + nrm((DEPTH, HEAD_DIM), 0.02),
        'k_gain': 1.0 + nrm((DEPTH, HEAD_DIM), 0.02),
        'ssm_lambda_re': -0.5 + nrm((DEPTH, 2, G, P), 0.01),
        'ssm_lambda_im': lam_im + nrm((DEPTH, 2, G, P), 0.01),
        'ssm_log_step': log_step,
        'ssm_b_re': nrm((DEPTH, 2, G, P, H), (2 * H) ** -0.5),
        'ssm_b_im': nrm((DEPTH, 2, G, P, H), (2 * H) ** -0.5),
        'ssm_c_re': nrm((DEPTH, 2, G, H, P), (2 * P) ** -0.5),
        'ssm_c_im': nrm((DEPTH, 2, G, H, P), (2 * P) ** -0.5),
        'ssm_d': nrm((DEPTH, SSM_WIDTH)),
        'w_glu': nrm((DEPTH, SSM_WIDTH, SSM_WIDTH), SSM_WIDTH ** -0.5),
        'b_glu': nrm((DEPTH, SSM_WIDTH), 0.02),
        'w_out': nrm((DEPTH, D_MODEL, D_MODEL), D_MODEL ** -0.5 * DEEPNORM_BETA),
        'ln1_g': 1.0 + nrm((DEPTH, D_MODEL), 0.02),
        'ln1_b': nrm((DEPTH, D_MODEL), 0.02),
        'ln2_g': 1.0 + nrm((DEPTH, D_MODEL), 0.02),
        'ln2_b': nrm((DEPTH, D_MODEL), 0.02),
        'ffn_w_gate': nrm((N_DENSE, D_MODEL, D_FF), D_MODEL ** -0.5),
        'ffn_w_up': nrm((N_DENSE, D_MODEL, D_FF), D_MODEL ** -0.5),
        'ffn_w_down': nrm((N_DENSE, D_FF, D_MODEL), D_FF ** -0.5 * DEEPNORM_BETA),
        'router_w': nrm((N_MOE, D_MODEL, N_EXPERTS), D_MODEL ** -0.5),
        'moe_w_gate': nrm((N_MOE, N_EXPERTS, D_MODEL, EXPERT_FF), D_MODEL ** -0.5),
        'moe_w_up': nrm((N_MOE, N_EXPERTS, D_MODEL, EXPERT_FF), D_MODEL ** -0.5),
        'moe_w_down': nrm((N_MOE, N_EXPERTS, EXPERT_FF, D_MODEL), EXPERT_FF ** -0.5 * DEEPNORM_BETA),
    }


def reference(x_prompt, x_sample, c, cache_k, cache_v, state_ssm_re, state_ssm_im, c_ctx, w_ada, b_ada,
              w_in, q_gain, k_gain, ssm_lambda_re, ssm_lambda_im, ssm_log_step, ssm_b_re, ssm_b_im,
              ssm_c_re, ssm_c_im, ssm_d, w_glu, b_glu, w_out, ln1_g, ln1_b, ln2_g, ln2_b,
              ffn_w_gate, ffn_w_up, ffn_w_down, router_w, moe_w_gate, moe_w_up, moe_w_down):
    rope = _axial_rope_tables(x_sample.shape[1])
    zero_state = jnp.zeros((x_prompt.shape[0], 2, SSM_GROUPS, SSM_STATE), jnp.float32)
    xp, xs = x_prompt, x_sample
    new_k, new_v, new_re, new_im = [], [], [], []
    for l in range(DEPTH):
        p = dict(w_in=w_in[l], q_gain=q_gain[l], k_gain=k_gain[l],
                 lam_re=ssm_lambda_re[l], lam_im=ssm_lambda_im[l], log_step=ssm_log_step[l],
                 b_re=ssm_b_re[l], b_im=ssm_b_im[l], c_re=ssm_c_re[l], c_im=ssm_c_im[l], d_skip=ssm_d[l],
                 w_glu=w_glu[l], b_glu=b_glu[l], w_out=w_out[l],
                 ln1_g=ln1_g[l], ln1_b=ln1_b[l], ln2_g=ln2_g[l], ln2_b=ln2_b[l])
        if l % 2 == 0:
            p.update(ffn_gate=ffn_w_gate[l // 2], ffn_up=ffn_w_up[l // 2], ffn_down=ffn_w_down[l // 2])
        else:
            p.update(router=router_w[l // 2], moe_gate=moe_w_gate[l // 2], moe_up=moe_w_up[l // 2],
                     moe_down=moe_w_down[l // 2])
        mod_ctx = (jax.nn.silu(c_ctx) @ w_ada[l] + b_ada[l])[None, None, :]
        xp, k_l, v_l, s_re, s_im = _layer(xp, mod_ctx, p, None, None, zero_state, zero_state, None)
        new_k.append(k_l)
        new_v.append(v_l)
        new_re.append(s_re)
        new_im.append(s_im)
        mod_lat = (jax.nn.silu(c) @ w_ada[l] + b_ada[l])[:, None, :]
        xs, _, _, _, _ = _layer(xs, mod_lat, p, cache_k[:, l], cache_v[:, l],
                                state_ssm_re[:, l], state_ssm_im[:, l], rope)
    new_k_arr = jnp.stack(new_k, axis=1)
    new_v_arr = jnp.stack(new_v, axis=1)
    new_re_arr = jnp.stack(new_re, axis=1)
    new_im_arr = jnp.stack(new_im, axis=1)
    return (xp, xs, new_k_arr, new_v_arr, new_re_arr, new_im_arr)
```

```python
import functools
import math

import jax
import jax.numpy as jnp
from jax import lax
from jax.experimental import pallas as pl
from jax.experimental.pallas import tpu as pltpu

D_MODEL = 1024
BATCH = 16
SEQ = 256
DEPTH = 4
DEC_BATCH = 2
DEC_SEQ = 2048
PAST_LEN = 256
GRID_W = 64
N_HEADS = 8
N_KV_HEADS = 2
HEAD_DIM = 64
ATTN_WIDTH = N_HEADS * HEAD_DIM
KV_WIDTH = N_KV_HEADS * HEAD_DIM
SSM_WIDTH = D_MODEL - ATTN_WIDTH
SSM_GROUP_CH = 16
SSM_GROUPS = SSM_WIDTH // SSM_GROUP_CH
SSM_STATE = 64
IN_WIDTH = ATTN_WIDTH + 2 * KV_WIDTH + SSM_WIDTH
ROPE_THETA = 10000.0
ROPE_PAIRS = HEAD_DIM // 4
D_FF = 2752
N_EXPERTS = 8
EXPERT_FF = 3584
DEEPNORM_ALPHA = (2 * DEPTH) ** 0.25
NORM_EPS = 1e-6

F32 = jnp.float32
BF16 = jnp.bfloat16
HIGHEST = lax.Precision.HIGHEST
LANES = 128

P_ROWS = BATCH * SEQ
S_ROWS = DEC_BATCH * DEC_SEQ
ROWS = P_ROWS + S_ROWS
N_MOD = 8
GRP = N_HEADS // N_KV_HEADS
SCALE = HEAD_DIM ** -0.5

TM_PROJ = 512
ROPE_ID_ROWS = TM_PROJ
TQ_PROMPT = SEQ
TQ_SAMPLE = 128
SCAN_CHUNK = 256
SCAN_PITCH = SCAN_CHUNK + 4
N_GP = SSM_GROUPS // 2
GP_PER_CHUNK = LANES // (2 * SSM_GROUP_CH)
S_CHUNKS = DEC_SEQ // SCAN_CHUNK
TM_FFN = 1024
TF_DENSE = 256
TF_MOE = 512
MIB = 1024 * 1024


def _params(n_axes, vmem_mib):
    return pltpu.CompilerParams(dimension_semantics=("arbitrary",) * n_axes,
                                vmem_limit_bytes=vmem_mib * MIB)


def _layer_norm(x):
    mu = jnp.mean(x, axis=-1, keepdims=True)
    xc = x - mu
    var = jnp.mean(xc * xc, axis=-1, keepdims=True)
    return xc * lax.rsqrt(var + NORM_EPS)


def _sigmoid(x):
    return 1.0 / (1.0 + jnp.exp(-x))


def _mod_row(tile, tm):
    p_tiles = P_ROWS // tm
    return jnp.where(tile < p_tiles, 0, 1 + (tile - p_tiles) // (DEC_SEQ // tm))


def _mod_kernel(c_ref, w_ref, b_ref, o_ref):
    c = c_ref[...]
    a = c * _sigmoid(c)
    o_ref[0] = jnp.dot(a, w_ref[0], precision=HIGHEST, preferred_element_type=F32) + b_ref[0]


def _modulation(cvec, w_ada, b_ada):
    tn = 1536
    return pl.pallas_call(
        _mod_kernel,
        grid=(DEPTH, 6 * D_MODEL // tn),
        in_specs=[pl.BlockSpec((N_MOD, D_MODEL), lambda l, j: (0, 0)),
                  pl.BlockSpec((1, D_MODEL, tn), lambda l, j: (l, 0, j)),
                  pl.BlockSpec((1, 1, tn), lambda l, j: (l, 0, j))],
        out_specs=pl.BlockSpec((1, N_MOD, tn), lambda l, j: (l, 0, j)),
        out_shape=jax.ShapeDtypeStruct((DEPTH, N_MOD, 6 * D_MODEL), F32),
        compiler_params=_params(2, 32),
        name="adaln_mod",
    )(cvec, w_ada, b_ada.reshape(DEPTH, 1, 6 * D_MODEL))


def _mod_spec(layer, part, tm):
    return pl.BlockSpec((1, 1, D_MODEL), lambda i: (layer * N_MOD + _mod_row(i, tm), 0, part))


def _inproj_kernel(x_ref, sh_ref, sc_ref, w_ref, qg_ref, kg_ref, cos_ref, sin_ref,
                   q_ref, kr_ref, kn_ref, v_ref, u_ref, wb_ref):
    @pl.when(pl.program_id(0) == 0)
    def _():
        wb_ref[...] = w_ref[...].astype(BF16)

    h = _layer_norm(x_ref[...]) * (1.0 + sc_ref[0]) + sh_ref[0]
    proj = jnp.dot(h.astype(BF16), wb_ref[...], preferred_element_type=F32)

    tm = proj.shape[0]
    head_of = lambda idx: lax.shift_right_logical(idx, int(math.log2(HEAD_DIM)))
    r = head_of(lax.broadcasted_iota(jnp.int32, (LANES, LANES), 0))
    c = head_of(lax.broadcasted_iota(jnp.int32, (LANES, LANES), 1))
    seg = jnp.where(r == c, 1.0 / HEAD_DIM, 0.0).astype(F32)
    lane = lax.broadcasted_iota(jnp.int32, (tm, LANES), 1)
    first_half = (lane & (2 * ROPE_PAIRS - 1)) < ROPE_PAIRS
    cos = cos_ref[...]
    sin = sin_ref[...]

    def head_norm(t, gain):
        ms = jnp.dot(t * t, seg, precision=HIGHEST, preferred_element_type=F32)
        return t * lax.rsqrt(ms + NORM_EPS) * gain

    def rope(t):
        partner = jnp.where(first_half, pltpu.roll(t, LANES - ROPE_PAIRS, 1), pltpu.roll(t, ROPE_PAIRS, 1))
        return t * cos + partner * sin

    for j in range(ATTN_WIDTH // LANES):
        sl = slice(j * LANES, (j + 1) * LANES)
        q_ref[:, sl] = rope(head_norm(proj[:, sl], qg_ref[...])) * SCALE
    k = head_norm(proj[:, ATTN_WIDTH:ATTN_WIDTH + KV_WIDTH], kg_ref[...])
    kn_ref[...] = k
    kr_ref[...] = rope(k)
    v_ref[...] = proj[:, ATTN_WIDTH + KV_WIDTH:ATTN_WIDTH + 2 * KV_WIDTH]
    u_ref[...] = proj[:, ATTN_WIDTH + 2 * KV_WIDTH:]


def _in_projection(layer, x, mod3, w_in, q_gain, k_gain, rope_cos, rope_sin):
    tm = TM_PROJ
    p_tiles = P_ROWS // tm
    t_tiles = DEC_SEQ // tm

    def rope_idx(i):
        return (jnp.where(i < p_tiles, 0, 1 + (i - p_tiles) % t_tiles), 0)

    row = lambda w: pl.BlockSpec((tm, w), lambda i: (i, 0))
    full = lambda a: pl.BlockSpec(a.shape, lambda i: (0,) * a.ndim)
    return pl.pallas_call(
        _inproj_kernel,
        grid=(ROWS // tm,),
        in_specs=[row(D_MODEL), _mod_spec(layer, 0, tm), _mod_spec(layer, 1, tm), full(w_in),
                  full(q_gain), full(k_gain),
                  pl.BlockSpec((tm, LANES), rope_idx), pl.BlockSpec((tm, LANES), rope_idx)],
        out_specs=[row(ATTN_WIDTH), row(KV_WIDTH), row(KV_WIDTH), row(KV_WIDTH), row(SSM_WIDTH)],
        out_shape=[jax.ShapeDtypeStruct((ROWS, ATTN_WIDTH), F32),
                   jax.ShapeDtypeStruct((ROWS, KV_WIDTH), F32),
                   jax.ShapeDtypeStruct((ROWS, KV_WIDTH), F32),
                   jax.ShapeDtypeStruct((ROWS, KV_WIDTH), F32),
                   jax.ShapeDtypeStruct((ROWS, SSM_WIDTH), F32)],
        scratch_shapes=[pltpu.VMEM((D_MODEL, IN_WIDTH), BF16)],
        compiler_params=_params(1, 48),
        name="in_projection",
    )(x, mod3, mod3, w_in, q_gain, k_gain, rope_cos, rope_sin)


def _attn_kernel(*refs, tq, has_ctx):
    if has_ctx:
        q_ref, k_ref, v_ref, kc_ref, vc_ref, _, o_ref = refs
    else:
        q_ref, k_ref, v_ref, o_ref = refs
    nt = (((1,), (1,)), ((), ()))
    for kv in range(N_KV_HEADS):
        sl = slice(kv * HEAD_DIM, (kv + 1) * HEAD_DIM)
        k = k_ref[:, sl].astype(BF16)
        v = v_ref[:, sl].astype(BF16)
        heads = [q_ref[:, (kv * GRP + g) * HEAD_DIM:(kv * GRP + g + 1) * HEAD_DIM] for g in range(GRP)]
        qs = jnp.concatenate(heads, axis=0).astype(BF16)
        s = lax.dot_general(qs, k, nt, preferred_element_type=F32)
        m = jnp.max(s, axis=-1, keepdims=True)
        if has_ctx:
            s_ctx = lax.dot_general(qs, kc_ref[:, sl].astype(BF16), nt, preferred_element_type=F32)
            m = jnp.maximum(m, jnp.max(s_ctx, axis=-1, keepdims=True))
        p = jnp.exp(s - m)
        den = jnp.sum(p, axis=-1, keepdims=True)
        o = jnp.dot(p.astype(BF16), v, preferred_element_type=F32)
        if has_ctx:
            p_ctx = jnp.exp(s_ctx - m)
            den = den + jnp.sum(p_ctx, axis=-1, keepdims=True)
            o = o + jnp.dot(p_ctx.astype(BF16), vc_ref[:, sl].astype(BF16), preferred_element_type=F32)
        o = o / den
        for g in range(GRP):
            h = kv * GRP + g
            o_ref[:, h * HEAD_DIM:(h + 1) * HEAD_DIM] = o[g * tq:(g + 1) * tq]


def _attention(layer, q, k_rot, v, cache_k4, cache_v4):
    tq = TQ_PROMPT
    blk = lambda w: pl.BlockSpec((tq, w), lambda b: (b, 0))
    attn = pl.pallas_call(
        functools.partial(_attn_kernel, tq=tq, has_ctx=False),
        grid=(BATCH,),
        in_specs=[blk(ATTN_WIDTH), blk(KV_WIDTH), blk(KV_WIDTH)],
        out_specs=blk(ATTN_WIDTH),
        out_shape=jax.ShapeDtypeStruct((ROWS, ATTN_WIDTH), F32),
        compiler_params=_params(1, 32),
        name="attn_prompt",
    )(q, k_rot, v)

    tq = TQ_SAMPLE
    q_tiles = DEC_SEQ // tq
    q_idx = lambda b, j: (P_ROWS // tq + b * q_tiles + j, 0)
    own = pl.BlockSpec((DEC_SEQ, KV_WIDTH), lambda b, j: (P_ROWS // DEC_SEQ + b, 0))
    ctx = pl.BlockSpec((None, None, PAST_LEN, KV_WIDTH), lambda b, j: (b, layer, 0, 0))
    return pl.pallas_call(
        functools.partial(_attn_kernel, tq=tq, has_ctx=True),
        grid=(DEC_BATCH, q_tiles),
        in_specs=[pl.BlockSpec((tq, ATTN_WIDTH), q_idx), own, own, ctx, ctx,
                  pl.BlockSpec(memory_space=pl.ANY)],
        out_specs=pl.BlockSpec((tq, ATTN_WIDTH), q_idx),
        out_shape=jax.ShapeDtypeStruct((ROWS, ATTN_WIDTH), F32),
        input_output_aliases={5: 0},
        compiler_params=_params(2, 48),
        name="attn_sample",
    )(q, k_rot, v, cache_k4, cache_v4, attn)


def _scan_item(i):
    j = i - BATCH
    seq = jnp.where(i < BATCH, i, BATCH + j // S_CHUNKS)
    bwd = jnp.where(i < BATCH, i, BATCH + (j // S_CHUNKS) * S_CHUNKS + (S_CHUNKS - 1) - j % S_CHUNKS)
    return seq, i, bwd


def _s5_kernel(uf_ref, ub_ref, h0_ref, lam_ref, b_ref, c_ref, yf_ref, yb_ref, fin_ref, slab, carry):
    tc, pitch = SCAN_CHUNK, SCAN_PITCH
    i = pl.program_id(0)

    @pl.when((i < BATCH) | ((i - BATCH) % S_CHUNKS == 0))
    def _():
        carry[...] = h0_ref[0]

    def section(d, part):
        return (2 * d + part) * N_GP * pitch

    for d, u_ref in ((0, uf_ref), (1, ub_ref)):
        for ch in range(SSM_WIDTH // LANES):
            u = u_ref[:, ch * LANES:(ch + 1) * LANES].astype(BF16)
            for gp in range(ch * GP_PER_CHUNK, (ch + 1) * GP_PER_CHUNK):
                bu = jnp.dot(u, b_ref[d, gp], preferred_element_type=F32)
                slab[pl.ds(section(d, 0) + gp * pitch, tc), :] = bu[:, :LANES]
                slab[pl.ds(section(d, 1) + gp * pitch, tc), :] = bu[:, LANES:]

    lam = [[lam_ref[d, part] for part in range(2)] for d in range(2)]

    def step(t, h):
        out = []
        for d in range(2):
            tt = t if d == 0 else tc - 1 - t
            rows_re = pl.ds(section(d, 0) + tt, N_GP, stride=pitch)
            rows_im = pl.ds(section(d, 1) + tt, N_GP, stride=pitch)
            ar, ai = lam[d]
            hr, hi = h[2 * d], h[2 * d + 1]
            nr = ar * hr - ai * hi + slab[rows_re, :]
            ni = ar * hi + ai * hr + slab[rows_im, :]
            slab[rows_re, :] = nr
            slab[rows_im, :] = ni
            out += [nr, ni]
        return tuple(out)

    h = lax.fori_loop(0, tc, step, tuple(carry[s] for s in range(4)))
    for s in range(4):
        carry[s] = h[s]
        fin_ref[0, s] = h[s]

    for d, y_ref in ((0, yf_ref), (1, yb_ref)):
        for ch in range(SSM_WIDTH // LANES):
            y = None
            for gp in range(ch * GP_PER_CHUNK, (ch + 1) * GP_PER_CHUNK):
                hs = jnp.concatenate([slab[pl.ds(section(d, 0) + gp * pitch, tc), :],
                                      slab[pl.ds(section(d, 1) + gp * pitch, tc), :]], axis=1).astype(BF16)
                part = jnp.dot(hs, c_ref[d, gp], preferred_element_type=F32)
                y = part if y is None else y + part
            y_ref[:, ch * LANES:(ch + 1) * LANES] = y


def _s5_scan(u, h0, lam_bar, b_pad, c_pad):
    tc = SCAN_CHUNK
    n_items = BATCH + DEC_BATCH * S_CHUNKS
    n_seq = BATCH + DEC_BATCH
    fwd = pl.BlockSpec((tc, SSM_WIDTH), lambda i: (_scan_item(i)[1], 0))
    bwd = pl.BlockSpec((tc, SSM_WIDTH), lambda i: (_scan_item(i)[2], 0))
    state = pl.BlockSpec((1, 4, N_GP, LANES), lambda i: (_scan_item(i)[0], 0, 0, 0))
    full = lambda a: pl.BlockSpec(a.shape, lambda i: (0,) * a.ndim)
    return pl.pallas_call(
        _s5_kernel,
        grid=(n_items,),
        in_specs=[fwd, bwd, state, full(lam_bar), full(b_pad), full(c_pad)],
        out_specs=[fwd, bwd, state],
        out_shape=[jax.ShapeDtypeStruct((ROWS, SSM_WIDTH), F32),
                   jax.ShapeDtypeStruct((ROWS, SSM_WIDTH), F32),
                   jax.ShapeDtypeStruct((n_seq, 4, N_GP, LANES), F32)],
        scratch_shapes=[pltpu.VMEM((4 * N_GP * SCAN_PITCH, LANES), F32),
                        pltpu.VMEM((4, N_GP, LANES), F32)],
        compiler_params=_params(1, 40),
        name="s5_scan",
    )(u, u, h0, lam_bar, b_pad, c_pad)


def _s5_discretise(lam_re, lam_im, log_step, b_re, b_im, c_re, c_im):
    delta = jnp.exp(log_step)[..., None]
    mag = jnp.exp(lam_re * delta)
    lbr = mag * jnp.cos(lam_im * delta)
    lbi = mag * jnp.sin(lam_im * delta)
    den = lam_re * lam_re + lam_im * lam_im
    cr = ((lbr - 1.0) * lam_re + lbi * lam_im) / den
    ci = (lbi * lam_re - (lbr - 1.0) * lam_im) / den
    bbr = cr[..., None] * b_re - ci[..., None] * b_im
    bbi = cr[..., None] * b_im + ci[..., None] * b_re
    lam_bar = jnp.stack([lbr, lbi], axis=2).reshape(DEPTH, 2, 2, N_GP, LANES)

    eye2 = jnp.eye(2, dtype=F32)
    slot = jax.nn.one_hot(jnp.arange(N_GP) % GP_PER_CHUNK, GP_PER_CHUNK, dtype=F32)
    bb = jnp.stack([bbr, bbi], axis=0).reshape(2, DEPTH, 2, N_GP, 2, SSM_STATE, SSM_GROUP_CH)
    b_small = jnp.einsum('cldgjph,jk->ldgjhckp', bb, eye2).reshape(DEPTH, 2, N_GP, 2 * SSM_GROUP_CH, 2 * LANES)
    b_pad = jnp.einsum('ldgrc,gs->ldgsrc', b_small, slot).reshape(DEPTH, 2, N_GP, LANES, 2 * LANES)
    cc = jnp.stack([c_re, -c_im], axis=0).reshape(2, DEPTH, 2, N_GP, 2, SSM_GROUP_CH, SSM_STATE)
    c_small = jnp.einsum('cldgjhp,jk->ldgckpjh', cc, eye2).reshape(DEPTH, 2, N_GP, 2 * LANES, 2 * SSM_GROUP_CH)
    c_pad = jnp.einsum('ldgrc,gs->ldgrsc', c_small, slot).reshape(DEPTH, 2, N_GP, 2 * LANES, LANES)
    return lam_bar, b_pad.astype(BF16), c_pad.astype(BF16)


def _outproj_kernel(*refs, moe):
    (attn_ref, yf_ref, yb_ref, u_ref, x_ref, g1_ref, sh2_ref, sc2_ref, d_ref, wglu_ref, bglu_ref,
     wout_ref, lng_ref, lnb_ref) = refs[:14]
    if moe:
        router_ref, x1_ref, h2_ref, gates_ref, wglu_b, wout_b = refs[14:]
    else:
        x1_ref, h2_ref, wglu_b, wout_b = refs[14:]

    @pl.when(pl.program_id(0) == 0)
    def _():
        wglu_b[...] = wglu_ref[...].astype(BF16)
        wout_b[...] = wout_ref[...].astype(BF16)

    y = yf_ref[...] + yb_ref[...] + d_ref[...] * u_ref[...]
    g = 0.5 * y * (1.0 + jnp.tanh(math.sqrt(2.0 / math.pi) * (y + 0.044715 * (y * y * y))))
    z = jnp.dot(g.astype(BF16), wglu_b[...], preferred_element_type=F32) + bglu_ref[...]
    y_ssm = g * _sigmoid(z)
    mix = (jnp.dot(attn_ref[...].astype(BF16), wout_b[:ATTN_WIDTH, :], preferred_element_type=F32)
           + jnp.dot(y_ssm.astype(BF16), wout_b[ATTN_WIDTH:, :], preferred_element_type=F32))
    x1 = _layer_norm(DEEPNORM_ALPHA * x_ref[...] + g1_ref[0] * mix) * lng_ref[...] + lnb_ref[...]
    x1_ref[...] = x1
    h2 = _layer_norm(x1) * (1.0 + sc2_ref[0]) + sh2_ref[0]
    h2_ref[...] = h2

    if moe:
        logits = jnp.dot(h2, router_ref[...], precision=HIGHEST, preferred_element_type=F32)
        lane = lax.broadcasted_iota(jnp.int32, logits.shape, 1).astype(F32)
        neg = jnp.float32(-jnp.inf)
        l1 = jnp.where(lane < N_EXPERTS, logits, neg)
        m1 = jnp.max(l1, axis=-1, keepdims=True)
        i1 = jnp.min(jnp.where(l1 == m1, lane, float(LANES)), axis=-1, keepdims=True)
        l2 = jnp.where(lane == i1, neg, l1)
        m2 = jnp.max(l2, axis=-1, keepdims=True)
        i2 = jnp.min(jnp.where(l2 == m2, lane, float(LANES)), axis=-1, keepdims=True)
        e2 = jnp.exp(m2 - m1)
        w1 = 1.0 / (1.0 + e2)
        w2 = e2 / (1.0 + e2)
        gates_ref[...] = jnp.where(lane == i1, w1, 0.0) + jnp.where(lane == i2, w2, 0.0)


def _out_projection(layer, attn, yf, yb, u, x, mod3, d_skip, w_glu, b_glu, w_out, ln_g, ln_b, router):
    tm = TM_PROJ
    moe = router is not None
    row = lambda w: pl.BlockSpec((tm, w), lambda i: (i, 0))
    full = lambda a: pl.BlockSpec(a.shape, lambda i: (0,) * a.ndim)
    args = [attn, yf, yb, u, x, mod3, mod3, mod3, d_skip, w_glu, b_glu, w_out, ln_g, ln_b]
    in_specs = [row(ATTN_WIDTH), row(SSM_WIDTH), row(SSM_WIDTH), row(SSM_WIDTH), row(D_MODEL),
                _mod_spec(layer, 2, tm), _mod_spec(layer, 3, tm), _mod_spec(layer, 4, tm),
                full(d_skip), full(w_glu), full(b_glu), full(w_out), full(ln_g), full(ln_b)]
    out_specs = [row(D_MODEL), row(D_MODEL)]
    out_shape = [jax.ShapeDtypeStruct((ROWS, D_MODEL), F32), jax.ShapeDtypeStruct((ROWS, D_MODEL), F32)]
    if moe:
        args.append(router)
        in_specs.append(full(router))
        out_specs.append(row(LANES))
        out_shape.append(jax.ShapeDtypeStruct((ROWS, LANES), F32))
    return pl.pallas_call(
        functools.partial(_outproj_kernel, moe=moe),
        grid=(ROWS // tm,),
        in_specs=in_specs,
        out_specs=out_specs,
        out_shape=out_shape,
        scratch_shapes=[pltpu.VMEM((SSM_WIDTH, SSM_WIDTH), BF16), pltpu.VMEM((D_MODEL, D_MODEL), BF16)],
        compiler_params=_params(1, 48),
        name="out_projection",
    )(*args)


def _ffn_kernel(*refs, moe, tf, d_ff):
    if moe:
        h_ref, x1_ref, g2_ref, gates_ref, wg_ref, wu_ref, wd_ref, lng_ref, lnb_ref, o_ref, hb, acc = refs
    else:
        h_ref, x1_ref, g2_ref, wg_ref, wu_ref, wd_ref, lng_ref, lnb_ref, o_ref, hb, acc = refs
    e, j = pl.program_id(1), pl.program_id(2)
    n_e, n_j = pl.num_programs(1), pl.num_programs(2)

    @pl.when((e == 0) & (j == 0))
    def _():
        hb[...] = h_ref[...].astype(BF16)
        acc[...] = jnp.zeros_like(acc)

    a = jnp.dot(hb[...], wg_ref[...].astype(BF16), preferred_element_type=F32)
    b = jnp.dot(hb[...], wu_ref[...].astype(BF16), preferred_element_type=F32)
    act = a * _sigmoid(a) * b
    wd = wd_ref[...]
    if moe:
        lane = lax.broadcasted_iota(jnp.int32, gates_ref.shape, 1)
        gate = jnp.sum(jnp.where(lane == e, gates_ref[...], 0.0), axis=-1, keepdims=True)
        act = act * gate
    if d_ff % tf:
        col = j * tf + lax.broadcasted_iota(jnp.int32, act.shape, 1)
        act = jnp.where(col < d_ff, act, 0.0)
        r = j * tf + lax.broadcasted_iota(jnp.int32, wd.shape, 0)
        wd = jnp.where(r < d_ff, wd, 0.0)
    acc[...] += jnp.dot(act.astype(BF16), wd.astype(BF16), preferred_element_type=F32)

    @pl.when((e == n_e - 1) & (j == n_j - 1))
    def _():
        y = DEEPNORM_ALPHA * x1_ref[...] + g2_ref[0] * acc[...]
        o_ref[...] = _layer_norm(y) * lng_ref[...] + lnb_ref[...]


def _ffn(layer, h2, x1, mod3, gates, wg, wu, wd, ln_g, ln_b):
    tm = TM_FFN
    moe = gates is not None
    w_idx = layer // 2
    if moe:
        tf, d_ff, n_e = TF_MOE, EXPERT_FF, N_EXPERTS
        up = pl.BlockSpec((None, None, D_MODEL, tf), lambda i, e, j: (w_idx, e, 0, j))
        down = pl.BlockSpec((None, None, tf, D_MODEL), lambda i, e, j: (w_idx, e, j, 0))
    else:
        tf, d_ff, n_e = TF_DENSE, D_FF, 1
        up = pl.BlockSpec((None, D_MODEL, tf), lambda i, e, j: (w_idx, 0, j))
        down = pl.BlockSpec((None, tf, D_MODEL), lambda i, e, j: (w_idx, j, 0))
    row = lambda w: pl.BlockSpec((tm, w), lambda i, e, j: (i, 0))
    vec = pl.BlockSpec((1, D_MODEL), lambda i, e, j: (0, 0))
    g2 = pl.BlockSpec((1, 1, D_MODEL), lambda i, e, j: (layer * N_MOD + _mod_row(i, tm), 0, 5))
    args = [h2, x1, mod3] + ([gates] if moe else []) + [wg, wu, wd, ln_g, ln_b]
    in_specs = [row(D_MODEL), row(D_MODEL), g2] + ([row(LANES)] if moe else []) + [up, up, down, vec, vec]
    return pl.pallas_call(
        functools.partial(_ffn_kernel, moe=moe, tf=tf, d_ff=d_ff),
        grid=(ROWS // tm, n_e, pl.cdiv(d_ff, tf)),
        in_specs=in_specs,
        out_specs=row(D_MODEL),
        out_shape=jax.ShapeDtypeStruct((ROWS, D_MODEL), F32),
        scratch_shapes=[pltpu.VMEM((tm, D_MODEL), BF16), pltpu.VMEM((tm, D_MODEL), F32)],
        compiler_params=_params(3, 56),
        name="ffn_moe" if moe else "ffn_dense",
    )(*args)


def _rope_tables():
    rows = DEC_SEQ // GRID_W
    row = jnp.repeat(jnp.arange(rows), GRID_W).astype(F32)
    col = jnp.tile(jnp.arange(GRID_W), rows).astype(F32)
    freqs = ROPE_THETA ** (-jnp.arange(ROPE_PAIRS, dtype=F32) / ROPE_PAIRS)
    ang = jnp.stack([row[:, None] * freqs, col[:, None] * freqs], axis=1)
    cos, sin = jnp.cos(ang), jnp.sin(ang)
    cos_h = jnp.concatenate([cos[:, 0], cos[:, 0], cos[:, 1], cos[:, 1]], axis=-1)
    sin_h = jnp.concatenate([-sin[:, 0], sin[:, 0], -sin[:, 1], sin[:, 1]], axis=-1)
    per_tile = LANES // HEAD_DIM
    cos_t = jnp.concatenate([jnp.ones((ROPE_ID_ROWS, LANES), F32), jnp.tile(cos_h, (1, per_tile))], axis=0)
    sin_t = jnp.concatenate([jnp.zeros((ROPE_ID_ROWS, LANES), F32), jnp.tile(sin_h, (1, per_tile))], axis=0)
    return cos_t, sin_t


def kernel(x_prompt, x_sample, c, cache_k, cache_v, state_ssm_re, state_ssm_im, c_ctx, w_ada, b_ada,
           w_in, q_gain, k_gain, ssm_lambda_re, ssm_lambda_im, ssm_log_step, ssm_b_re, ssm_b_im,
           ssm_c_re, ssm_c_im, ssm_d, w_glu, b_glu, w_out, ln1_g, ln1_b, ln2_g, ln2_b,
           ffn_w_gate, ffn_w_up, ffn_w_down, router_w, moe_w_gate, moe_w_up, moe_w_down):
    x = jnp.concatenate([x_prompt.reshape(P_ROWS, D_MODEL), x_sample.reshape(S_ROWS, D_MODEL)], axis=0)
    cvec = jnp.concatenate([c_ctx[None], c, jnp.zeros((N_MOD - 1 - DEC_BATCH, D_MODEL), F32)], axis=0)
    mod3 = _modulation(cvec, w_ada, b_ada).reshape(DEPTH * N_MOD, 1, 6 * D_MODEL)
    rope_cos, rope_sin = _rope_tables()
    lam_bar, b_pad, c_pad = _s5_discretise(ssm_lambda_re, ssm_lambda_im, ssm_log_step,
                                           ssm_b_re, ssm_b_im, ssm_c_re, ssm_c_im)
    cache_k4 = cache_k.reshape(DEC_BATCH, DEPTH, PAST_LEN, KV_WIDTH)
    cache_v4 = cache_v.reshape(DEC_BATCH, DEPTH, PAST_LEN, KV_WIDTH)
    h0_s = jnp.stack([state_ssm_re, state_ssm_im], axis=3)
    h0_s = h0_s.reshape(DEC_BATCH, DEPTH, 4, N_GP, LANES)
    h0_all = jnp.concatenate([jnp.zeros((BATCH, DEPTH, 4, N_GP, LANES), F32), h0_s], axis=0)
    router_pad = jnp.pad(router_w, ((0, 0), (0, 0), (0, LANES - N_EXPERTS)))
    gain2 = lambda g: jnp.tile(g, (1, LANES // HEAD_DIM))

    new_k, new_v, new_re, new_im = [], [], [], []
    for l in range(DEPTH):
        q, k_rot, k_norm, v, u = _in_projection(l, x, mod3, w_in[l], gain2(q_gain[l:l + 1]),
                                                gain2(k_gain[l:l + 1]), rope_cos, rope_sin)
        attn = _attention(l, q, k_rot, v, cache_k4, cache_v4)
        yf, yb, fin = _s5_scan(u, h0_all[:, l], lam_bar[l], b_pad[l], c_pad[l])
        moe = l % 2 == 1
        outs = _out_projection(l, attn, yf, yb, u, x, mod3, ssm_d[l:l + 1], w_glu[l], b_glu[l:l + 1],
                               w_out[l], ln1_g[l:l + 1], ln1_b[l:l + 1], router_pad[l // 2] if moe else None)
        if moe:
            x1, h2, gates = outs
            x = _ffn(l, h2, x1, mod3, gates, moe_w_gate, moe_w_up, moe_w_down, ln2_g[l:l + 1], ln2_b[l:l + 1])
        else:
            x1, h2 = outs
            x = _ffn(l, h2, x1, mod3, None, ffn_w_gate, ffn_w_up, ffn_w_down, ln2_g[l:l + 1], ln2_b[l:l + 1])
        new_k.append(k_norm[:P_ROWS].reshape(BATCH, SEQ, N_KV_HEADS, HEAD_DIM))
        new_v.append(v[:P_ROWS].reshape(BATCH, SEQ, N_KV_HEADS, HEAD_DIM))
        fin_p = fin[:BATCH].reshape(BATCH, 2, 2, SSM_GROUPS, SSM_STATE)
        new_re.append(fin_p[:, :, 0])
        new_im.append(fin_p[:, :, 1])

    y_prompt = x[:P_ROWS].reshape(BATCH, SEQ, D_MODEL)
    y_sample = x[P_ROWS:].reshape(DEC_BATCH, DEC_SEQ, D_MODEL)
    return (y_prompt, y_sample, jnp.stack(new_k, axis=1), jnp.stack(new_v, axis=1),
            jnp.stack(new_re, axis=1), jnp.stack(new_im, axis=1))
```

```python
import functools
import math

import jax
import jax.numpy as jnp
from jax import lax
from jax.experimental import pallas as pl
from jax.experimental.pallas import tpu as pltpu

D_MODEL = 1024
BATCH = 16
SEQ = 256
DEPTH = 4
DEC_BATCH = 2
DEC_SEQ = 2048
PAST_LEN = 256
GRID_W = 64
N_HEADS = 8
N_KV_HEADS = 2
HEAD_DIM = 64
ATTN_WIDTH = N_HEADS * HEAD_DIM
KV_WIDTH = N_KV_HEADS * HEAD_DIM
SSM_WIDTH = D_MODEL - ATTN_WIDTH
SSM_GROUP_CH = 16
SSM_GROUPS = SSM_WIDTH // SSM_GROUP_CH
SSM_STATE = 64
IN_WIDTH = ATTN_WIDTH + 2 * KV_WIDTH + SSM_WIDTH
ROPE_THETA = 10000.0
ROPE_PAIRS = HEAD_DIM // 4
D_FF = 2752
N_EXPERTS = 8
TOP_K = 2
EXPERT_FF = 3584
DEEPNORM_ALPHA = (2 * DEPTH) ** 0.25
NORM_EPS = 1e-6

F32 = jnp.float32
BF16 = jnp.bfloat16
HIGHEST = lax.Precision.HIGHEST
LANES = 128

P_ROWS = BATCH * SEQ
S_ROWS = DEC_BATCH * DEC_SEQ
ROWS = P_ROWS + S_ROWS
N_MOD = 8
GRP = N_HEADS // N_KV_HEADS
SCALE = HEAD_DIM ** -0.5

TM_PROJ = 512
ROPE_ID_ROWS = TM_PROJ
TQ_PROMPT = SEQ
TQ_SAMPLE = 128
SCAN_CHUNK = 256
SCAN_PITCH = SCAN_CHUNK + 4
SCAN_UNROLL = 8
N_GP = SSM_GROUPS // 2
GP_PER_CHUNK = LANES // (2 * SSM_GROUP_CH)
S_CHUNKS = DEC_SEQ // SCAN_CHUNK
TM_FFN = 1024
TF_DENSE = 256
TF_MOE = 512
MOE_TILE = 1024
MOE_SUB = 256
MOE_TILES = TOP_K * ROWS // MOE_TILE + N_EXPERTS
MOE_ROWS = MOE_TILES * MOE_TILE
TM_ROUTE = 1024
TM_COMBINE = 512
MIB = 1024 * 1024


def _params(n_axes, vmem_mib):
    return pltpu.CompilerParams(dimension_semantics=("arbitrary",) * n_axes,
                                vmem_limit_bytes=vmem_mib * MIB)


def _layer_norm(x):
    mu = jnp.mean(x, axis=-1, keepdims=True)
    xc = x - mu
    var = jnp.mean(xc * xc, axis=-1, keepdims=True)
    return xc * lax.rsqrt(var + NORM_EPS)


def _sigmoid(x):
    return 1.0 / (1.0 + jnp.exp(-x))


def _mod_row(tile, tm):
    p_tiles = P_ROWS // tm
    return jnp.where(tile < p_tiles, 0, 1 + (tile - p_tiles) // (DEC_SEQ // tm))


def _mod_kernel(c_ref, w_ref, b_ref, o_ref):
    c = c_ref[...]
    a = c * _sigmoid(c)
    o_ref[0] = jnp.dot(a, w_ref[0], precision=HIGHEST, preferred_element_type=F32) + b_ref[0]


def _modulation(cvec, w_ada, b_ada):
    tn = 1536
    return pl.pallas_call(
        _mod_kernel,
        grid=(DEPTH, 6 * D_MODEL // tn),
        in_specs=[pl.BlockSpec((N_MOD, D_MODEL), lambda l, j: (0, 0)),
                  pl.BlockSpec((1, D_MODEL, tn), lambda l, j: (l, 0, j)),
                  pl.BlockSpec((1, 1, tn), lambda l, j: (l, 0, j))],
        out_specs=pl.BlockSpec((1, N_MOD, tn), lambda l, j: (l, 0, j)),
        out_shape=jax.ShapeDtypeStruct((DEPTH, N_MOD, 6 * D_MODEL), F32),
        compiler_params=_params(2, 32),
        name="adaln_mod",
    )(cvec, w_ada, b_ada.reshape(DEPTH, 1, 6 * D_MODEL))


def _mod_spec(layer, part, tm):
    return pl.BlockSpec((1, 1, D_MODEL), lambda i: (layer * N_MOD + _mod_row(i, tm), 0, part))


def _inproj_kernel(x_ref, sh_ref, sc_ref, w_ref, qg_ref, kg_ref, cos_ref, sin_ref,
                   q_ref, kr_ref, kn_ref, v_ref, u_ref, wb_ref):
    @pl.when(pl.program_id(0) == 0)
    def _():
        wb_ref[...] = w_ref[...].astype(BF16)

    h = _layer_norm(x_ref[...]) * (1.0 + sc_ref[0]) + sh_ref[0]
    proj = jnp.dot(h.astype(BF16), wb_ref[...], preferred_element_type=F32)

    tm = proj.shape[0]
    head_of = lambda idx: lax.shift_right_logical(idx, int(math.log2(HEAD_DIM)))
    r = head_of(lax.broadcasted_iota(jnp.int32, (LANES, LANES), 0))
    c = head_of(lax.broadcasted_iota(jnp.int32, (LANES, LANES), 1))
    seg = jnp.where(r == c, 1.0 / HEAD_DIM, 0.0).astype(F32)
    lane = lax.broadcasted_iota(jnp.int32, (tm, LANES), 1)
    first_half = (lane & (2 * ROPE_PAIRS - 1)) < ROPE_PAIRS
    cos = cos_ref[...]
    sin = sin_ref[...]

    def head_norm(t, gain):
        ms = jnp.dot(t * t, seg, precision=HIGHEST, preferred_element_type=F32)
        return t * lax.rsqrt(ms + NORM_EPS) * gain

    def rope(t):
        partner = jnp.where(first_half, pltpu.roll(t, LANES - ROPE_PAIRS, 1), pltpu.roll(t, ROPE_PAIRS, 1))
        return t * cos + partner * sin

    for j in range(ATTN_WIDTH // LANES):
        sl = slice(j * LANES, (j + 1) * LANES)
        q_ref[:, sl] = rope(head_norm(proj[:, sl], qg_ref[...])) * SCALE
    k = head_norm(proj[:, ATTN_WIDTH:ATTN_WIDTH + KV_WIDTH], kg_ref[...])
    kn_ref[...] = k
    kr_ref[...] = rope(k)
    v_ref[...] = proj[:, ATTN_WIDTH + KV_WIDTH:ATTN_WIDTH + 2 * KV_WIDTH]
    u_ref[...] = proj[:, ATTN_WIDTH + 2 * KV_WIDTH:]


def _in_projection(layer, x, mod3, w_in, q_gain, k_gain, rope_cos, rope_sin):
    tm = TM_PROJ
    p_tiles = P_ROWS // tm
    t_tiles = DEC_SEQ // tm

    def rope_idx(i):
        return (jnp.where(i < p_tiles, 0, 1 + (i - p_tiles) % t_tiles), 0)

    row = lambda w: pl.BlockSpec((tm, w), lambda i: (i, 0))
    full = lambda a: pl.BlockSpec(a.shape, lambda i: (0,) * a.ndim)
    return pl.pallas_call(
        _inproj_kernel,
        grid=(ROWS // tm,),
        in_specs=[row(D_MODEL), _mod_spec(layer, 0, tm), _mod_spec(layer, 1, tm), full(w_in),
                  full(q_gain), full(k_gain),
                  pl.BlockSpec((tm, LANES), rope_idx), pl.BlockSpec((tm, LANES), rope_idx)],
        out_specs=[row(ATTN_WIDTH), row(KV_WIDTH), row(KV_WIDTH), row(KV_WIDTH), row(SSM_WIDTH)],
        out_shape=[jax.ShapeDtypeStruct((ROWS, ATTN_WIDTH), F32),
                   jax.ShapeDtypeStruct((ROWS, KV_WIDTH), F32),
                   jax.ShapeDtypeStruct((ROWS, KV_WIDTH), F32),
                   jax.ShapeDtypeStruct((ROWS, KV_WIDTH), F32),
                   jax.ShapeDtypeStruct((ROWS, SSM_WIDTH), F32)],
        scratch_shapes=[pltpu.VMEM((D_MODEL, IN_WIDTH), BF16)],
        compiler_params=_params(1, 48),
        name="in_projection",
    )(x, mod3, mod3, w_in, q_gain, k_gain, rope_cos, rope_sin)


def _attn_kernel(*refs, tq, has_ctx):
    if has_ctx:
        q_ref, k_ref, v_ref, kc_ref, vc_ref, o_ref = refs
    else:
        q_ref, k_ref, v_ref, o_ref = refs
    nt = (((1,), (1,)), ((), ()))
    for kv in range(N_KV_HEADS):
        sl = slice(kv * HEAD_DIM, (kv + 1) * HEAD_DIM)
        k = k_ref[:, sl].astype(BF16)
        v = v_ref[:, sl].astype(BF16)
        heads = [q_ref[:, (kv * GRP + g) * HEAD_DIM:(kv * GRP + g + 1) * HEAD_DIM] for g in range(GRP)]
        qs = jnp.concatenate(heads, axis=0).astype(BF16)
        s = lax.dot_general(qs, k, nt, preferred_element_type=F32)
        m = jnp.max(s, axis=-1, keepdims=True)
        if has_ctx:
            s_ctx = lax.dot_general(qs, kc_ref[:, sl].astype(BF16), nt, preferred_element_type=F32)
            m = jnp.maximum(m, jnp.max(s_ctx, axis=-1, keepdims=True))
        p = jnp.exp(s - m)
        den = jnp.sum(p, axis=-1, keepdims=True)
        o = jnp.dot(p.astype(BF16), v, preferred_element_type=F32)
        if has_ctx:
            p_ctx = jnp.exp(s_ctx - m)
            den = den + jnp.sum(p_ctx, axis=-1, keepdims=True)
            o = o + jnp.dot(p_ctx.astype(BF16), vc_ref[:, sl].astype(BF16), preferred_element_type=F32)
        o = o / den
        for g in range(GRP):
            h = kv * GRP + g
            o_ref[:, h * HEAD_DIM:(h + 1) * HEAD_DIM] = o[g * tq:(g + 1) * tq]


def _attention(layer, q, k_rot, v, cache_k4, cache_v4):
    tq = TQ_PROMPT
    blk = lambda w: pl.BlockSpec((tq, w), lambda b: (b, 0))
    attn_p = pl.pallas_call(
        functools.partial(_attn_kernel, tq=tq, has_ctx=False),
        grid=(BATCH,),
        in_specs=[blk(ATTN_WIDTH), blk(KV_WIDTH), blk(KV_WIDTH)],
        out_specs=blk(ATTN_WIDTH),
        out_shape=jax.ShapeDtypeStruct((P_ROWS, ATTN_WIDTH), F32),
        compiler_params=_params(1, 32),
        name="attn_prompt",
    )(q, k_rot, v)

    tq = TQ_SAMPLE
    q_tiles = DEC_SEQ // tq
    own = pl.BlockSpec((DEC_SEQ, KV_WIDTH), lambda b, j: (P_ROWS // DEC_SEQ + b, 0))
    ctx = pl.BlockSpec((None, None, PAST_LEN, KV_WIDTH), lambda b, j: (b, layer, 0, 0))
    attn_s = pl.pallas_call(
        functools.partial(_attn_kernel, tq=tq, has_ctx=True),
        grid=(DEC_BATCH, q_tiles),
        in_specs=[pl.BlockSpec((tq, ATTN_WIDTH), lambda b, j: (P_ROWS // tq + b * q_tiles + j, 0)),
                  own, own, ctx, ctx],
        out_specs=pl.BlockSpec((tq, ATTN_WIDTH), lambda b, j: (b * q_tiles + j, 0)),
        out_shape=jax.ShapeDtypeStruct((S_ROWS, ATTN_WIDTH), F32),
        compiler_params=_params(2, 48),
        name="attn_sample",
    )(q, k_rot, v, cache_k4, cache_v4)
    return attn_p, attn_s


def _scan_item(i):
    j = i - BATCH
    seq = jnp.where(i < BATCH, i, BATCH + j // S_CHUNKS)
    bwd = jnp.where(i < BATCH, i, BATCH + (j // S_CHUNKS) * S_CHUNKS + (S_CHUNKS - 1) - j % S_CHUNKS)
    return seq, i, bwd


def _s5_kernel(uf_ref, ub_ref, h0_ref, lam_ref, b_ref, c_ref, yf_ref, yb_ref, fin_ref, drive, slab, carry):
    tc, pitch = SCAN_CHUNK, SCAN_PITCH
    i = pl.program_id(0)

    @pl.when((i < BATCH) | ((i - BATCH) % S_CHUNKS == 0))
    def _():
        carry[...] = h0_ref[0]

    def section(d, part):
        return (2 * d + part) * N_GP * pitch

    for d, u_ref in ((0, uf_ref), (1, ub_ref)):
        for ch in range(SSM_WIDTH // LANES):
            u = u_ref[:, ch * LANES:(ch + 1) * LANES].astype(BF16)
            for gp in range(ch * GP_PER_CHUNK, (ch + 1) * GP_PER_CHUNK):
                bu = jnp.dot(u, b_ref[d, gp], preferred_element_type=F32)
                drive[pl.ds(section(d, 0) + gp * pitch, tc), :] = bu[:, :LANES]
                drive[pl.ds(section(d, 1) + gp * pitch, tc), :] = bu[:, LANES:]

    lam = [[lam_ref[d, part] for part in range(2)] for d in range(2)]

    def step(t, h):
        out = []
        for d in range(2):
            tt = t if d == 0 else tc - 1 - t
            rows_re = pl.ds(section(d, 0) + tt, N_GP, stride=pitch)
            rows_im = pl.ds(section(d, 1) + tt, N_GP, stride=pitch)
            ar, ai = lam[d]
            hr, hi = h[2 * d], h[2 * d + 1]
            nr = ar * hr - ai * hi + drive[rows_re, :]
            ni = ar * hi + ai * hr + drive[rows_im, :]
            slab[rows_re, :] = nr
            slab[rows_im, :] = ni
            out += [nr, ni]
        return tuple(out)

    h = lax.fori_loop(0, tc, step, tuple(carry[s] for s in range(4)), unroll=SCAN_UNROLL)
    for s in range(4):
        carry[s] = h[s]
        fin_ref[0, s] = h[s]

    for d, y_ref in ((0, yf_ref), (1, yb_ref)):
        for ch in range(SSM_WIDTH // LANES):
            y = None
            for gp in range(ch * GP_PER_CHUNK, (ch + 1) * GP_PER_CHUNK):
                hs = jnp.concatenate([slab[pl.ds(section(d, 0) + gp * pitch, tc), :],
                                      slab[pl.ds(section(d, 1) + gp * pitch, tc), :]], axis=1).astype(BF16)
                part = jnp.dot(hs, c_ref[d, gp], preferred_element_type=F32)
                y = part if y is None else y + part
            y_ref[:, ch * LANES:(ch + 1) * LANES] = y


def _s5_scan(u, h0, lam_bar, b_pad, c_pad):
    tc = SCAN_CHUNK
    n_items = BATCH + DEC_BATCH * S_CHUNKS
    n_seq = BATCH + DEC_BATCH
    fwd = pl.BlockSpec((tc, SSM_WIDTH), lambda i: (_scan_item(i)[1], 0))
    bwd = pl.BlockSpec((tc, SSM_WIDTH), lambda i: (_scan_item(i)[2], 0))
    state = pl.BlockSpec((1, 4, N_GP, LANES), lambda i: (_scan_item(i)[0], 0, 0, 0))
    full = lambda a: pl.BlockSpec(a.shape, lambda i: (0,) * a.ndim)
    return pl.pallas_call(
        _s5_kernel,
        grid=(n_items,),
        in_specs=[fwd, bwd, state, full(lam_bar), full(b_pad), full(c_pad)],
        out_specs=[fwd, bwd, state],
        out_shape=[jax.ShapeDtypeStruct((ROWS, SSM_WIDTH), F32),
                   jax.ShapeDtypeStruct((ROWS, SSM_WIDTH), F32),
                   jax.ShapeDtypeStruct((n_seq, 4, N_GP, LANES), F32)],
        scratch_shapes=[pltpu.VMEM((4 * N_GP * SCAN_PITCH, LANES), F32),
                        pltpu.VMEM((4 * N_GP * SCAN_PITCH, LANES), F32),
                        pltpu.VMEM((4, N_GP, LANES), F32)],
        compiler_params=_params(1, 48),
        name="s5_scan",
    )(u, u, h0, lam_bar, b_pad, c_pad)


def _s5_discretise(lam_re, lam_im, log_step, b_re, b_im, c_re, c_im):
    delta = jnp.exp(log_step)[..., None]
    mag = jnp.exp(lam_re * delta)
    lbr = mag * jnp.cos(lam_im * delta)
    lbi = mag * jnp.sin(lam_im * delta)
    den = lam_re * lam_re + lam_im * lam_im
    cr = ((lbr - 1.0) * lam_re + lbi * lam_im) / den
    ci = (lbi * lam_re - (lbr - 1.0) * lam_im) / den
    bbr = cr[..., None] * b_re - ci[..., None] * b_im
    bbi = cr[..., None] * b_im + ci[..., None] * b_re
    lam_bar = jnp.stack([lbr, lbi], axis=2).reshape(DEPTH, 2, 2, N_GP, LANES)

    eye2 = jnp.eye(2, dtype=F32)
    slot = jax.nn.one_hot(jnp.arange(N_GP) % GP_PER_CHUNK, GP_PER_CHUNK, dtype=F32)
    bb = jnp.stack([bbr, bbi], axis=0).reshape(2, DEPTH, 2, N_GP, 2, SSM_STATE, SSM_GROUP_CH)
    b_small = jnp.einsum('cldgjph,jk->ldgjhckp', bb, eye2).reshape(DEPTH, 2, N_GP, 2 * SSM_GROUP_CH, 2 * LANES)
    b_pad = jnp.einsum('ldgrc,gs->ldgsrc', b_small, slot).reshape(DEPTH, 2, N_GP, LANES, 2 * LANES)
    cc = jnp.stack([c_re, -c_im], axis=0).reshape(2, DEPTH, 2, N_GP, 2, SSM_GROUP_CH, SSM_STATE)
    c_small = jnp.einsum('cldgjhp,jk->ldgckpjh', cc, eye2).reshape(DEPTH, 2, N_GP, 2 * LANES, 2 * SSM_GROUP_CH)
    c_pad = jnp.einsum('ldgrc,gs->ldgrsc', c_small, slot).reshape(DEPTH, 2, N_GP, 2 * LANES, LANES)
    return lam_bar, b_pad.astype(BF16), c_pad.astype(BF16)


def _outproj_kernel(*refs, moe):
    (attn_p_ref, attn_s_ref, yf_ref, yb_ref, u_ref, x_ref, g1_ref, sh2_ref, sc2_ref, d_ref, wglu_ref,
     bglu_ref, wout_ref, lng_ref, lnb_ref) = refs[:15]
    if moe:
        router_ref, x1_ref, h2_ref, meta_ref, cnt_ref, wglu_b, wout_b, cnt = refs[15:]
    else:
        x1_ref, h2_ref, wglu_b, wout_b = refs[15:]
    is_prompt = pl.program_id(0) < P_ROWS // x_ref.shape[0]
    attn = jnp.where(is_prompt, attn_p_ref[...], attn_s_ref[...])

    @pl.when(pl.program_id(0) == 0)
    def _():
        wglu_b[...] = wglu_ref[...].astype(BF16)
        wout_b[...] = wout_ref[...].astype(BF16)

    y = yf_ref[...] + yb_ref[...] + d_ref[...] * u_ref[...]
    g = 0.5 * y * (1.0 + jnp.tanh(math.sqrt(2.0 / math.pi) * (y + 0.044715 * (y * y * y))))
    z = jnp.dot(g.astype(BF16), wglu_b[...], preferred_element_type=F32) + bglu_ref[...]
    y_ssm = g * _sigmoid(z)
    mix = (jnp.dot(attn.astype(BF16), wout_b[:ATTN_WIDTH, :], preferred_element_type=F32)
           + jnp.dot(y_ssm.astype(BF16), wout_b[ATTN_WIDTH:, :], preferred_element_type=F32))
    x1 = _layer_norm(DEEPNORM_ALPHA * x_ref[...] + g1_ref[0] * mix) * lng_ref[...] + lnb_ref[...]
    x1_ref[...] = x1
    h2 = _layer_norm(x1) * (1.0 + sc2_ref[0]) + sh2_ref[0]
    h2_ref[...] = h2

    if moe:
        @pl.when(pl.program_id(0) == 0)
        def _():
            cnt[...] = jnp.zeros_like(cnt)

        logits = jnp.dot(h2, router_ref[...], precision=HIGHEST, preferred_element_type=F32)
        tm = logits.shape[0]
        lane = lax.broadcasted_iota(jnp.int32, logits.shape, 1).astype(F32)
        neg = jnp.float32(-jnp.inf)
        l1 = jnp.where(lane < N_EXPERTS, logits, neg)
        m1 = jnp.max(l1, axis=-1, keepdims=True)
        i1 = jnp.min(jnp.where(l1 == m1, lane, float(LANES)), axis=-1, keepdims=True)
        l2 = jnp.where(lane == i1, neg, l1)
        m2 = jnp.max(l2, axis=-1, keepdims=True)
        i2 = jnp.min(jnp.where(l2 == m2, lane, float(LANES)), axis=-1, keepdims=True)
        e2 = jnp.exp(m2 - m1)
        w1 = 1.0 / (1.0 + e2)
        w2 = e2 / (1.0 + e2)
        hit = jnp.where((lane == i1) | (lane == i2), 1.0, 0.0)
        r = lax.broadcasted_iota(jnp.int32, (tm, tm), 0)
        c = lax.broadcasted_iota(jnp.int32, (tm, tm), 1)
        earlier = jnp.where(c < r, 1.0, 0.0).astype(BF16)
        rank = cnt[...] + jnp.dot(earlier, hit.astype(BF16), preferred_element_type=F32)
        r1 = jnp.sum(jnp.where(lane == i1, rank, 0.0), axis=-1, keepdims=True)
        r2 = jnp.sum(jnp.where(lane == i2, rank, 0.0), axis=-1, keepdims=True)
        cnt[...] = cnt[...] + jnp.sum(hit, axis=0, keepdims=True)
        cnt_ref[...] = cnt[...]
        fields = (i1, i2, r1, r2, w1, w2)
        meta = jnp.zeros_like(logits)
        for k, f in enumerate(fields):
            meta = jnp.where(lane == float(k), f, meta)
        meta_ref[...] = meta


def _out_projection(layer, attn_p, attn_s, yf, yb, u, x, mod3, d_skip, w_glu, b_glu, w_out, ln_g, ln_b, router):
    tm = TM_PROJ
    moe = router is not None
    p_tiles = P_ROWS // tm
    row = lambda w: pl.BlockSpec((tm, w), lambda i: (i, 0))
    full = lambda a: pl.BlockSpec(a.shape, lambda i: (0,) * a.ndim)
    args = [attn_p, attn_s, yf, yb, u, x, mod3, mod3, mod3, d_skip, w_glu, b_glu, w_out, ln_g, ln_b]
    in_specs = [pl.BlockSpec((tm, ATTN_WIDTH), lambda i: (jnp.minimum(i, p_tiles - 1), 0)),
                pl.BlockSpec((tm, ATTN_WIDTH), lambda i: (jnp.maximum(i - p_tiles, 0), 0)),
                row(SSM_WIDTH), row(SSM_WIDTH), row(SSM_WIDTH), row(D_MODEL),
                _mod_spec(layer, 2, tm), _mod_spec(layer, 3, tm), _mod_spec(layer, 4, tm),
                full(d_skip), full(w_glu), full(b_glu), full(w_out), full(ln_g), full(ln_b)]
    out_specs = [row(D_MODEL), row(D_MODEL)]
    out_shape = [jax.ShapeDtypeStruct((ROWS, D_MODEL), F32), jax.ShapeDtypeStruct((ROWS, D_MODEL), F32)]
    scratch = [pltpu.VMEM((SSM_WIDTH, SSM_WIDTH), BF16), pltpu.VMEM((D_MODEL, D_MODEL), BF16)]
    if moe:
        args.append(router)
        in_specs.append(full(router))
        out_specs += [row(LANES), pl.BlockSpec((1, LANES), lambda i: (0, 0))]
        out_shape += [jax.ShapeDtypeStruct((ROWS, LANES), F32), jax.ShapeDtypeStruct((1, LANES), F32)]
        scratch.append(pltpu.VMEM((1, LANES), F32))
    return pl.pallas_call(
        functools.partial(_outproj_kernel, moe=moe),
        grid=(ROWS // tm,),
        in_specs=in_specs,
        out_specs=out_specs,
        out_shape=out_shape,
        scratch_shapes=scratch,
        compiler_params=_params(1, 48),
        name="out_projection",
    )(*args)


def _swiglu_partial(xb, wg, wu, wd):
    a = jnp.dot(xb, wg, preferred_element_type=F32)
    b = jnp.dot(xb, wu, preferred_element_type=F32)
    act = a * _sigmoid(a) * b
    return act, lambda act_: jnp.dot(act_.astype(BF16), wd, preferred_element_type=F32)


def _ffn_kernel(h_ref, x1_ref, g2_ref, wg_ref, wu_ref, wd_ref, lng_ref, lnb_ref, o_ref, hb, acc, *, tf, d_ff):
    j = pl.program_id(1)

    @pl.when(j == 0)
    def _():
        hb[...] = h_ref[...].astype(BF16)
        acc[...] = jnp.zeros_like(acc)

    wd = wd_ref[...]
    if d_ff % tf:
        r = j * tf + lax.broadcasted_iota(jnp.int32, wd.shape, 0)
        wd = jnp.where(r < d_ff, wd, 0.0)
    act, down = _swiglu_partial(hb[...], wg_ref[...].astype(BF16), wu_ref[...].astype(BF16), wd.astype(BF16))
    if d_ff % tf:
        col = j * tf + lax.broadcasted_iota(jnp.int32, act.shape, 1)
        act = jnp.where(col < d_ff, act, 0.0)
    acc[...] += down(act)

    @pl.when(j == pl.num_programs(1) - 1)
    def _():
        y = DEEPNORM_ALPHA * x1_ref[...] + g2_ref[0] * acc[...]
        o_ref[...] = _layer_norm(y) * lng_ref[...] + lnb_ref[...]


def _ffn(layer, h2, x1, mod3, wg, wu, wd, ln_g, ln_b):
    tm, tf, d_ff = TM_FFN, TF_DENSE, D_FF
    w_idx = layer // 2
    up = pl.BlockSpec((None, D_MODEL, tf), lambda i, j: (w_idx, 0, j))
    down = pl.BlockSpec((None, tf, D_MODEL), lambda i, j: (w_idx, j, 0))
    row = lambda w: pl.BlockSpec((tm, w), lambda i, j: (i, 0))
    vec = pl.BlockSpec((1, D_MODEL), lambda i, j: (0, 0))
    g2 = pl.BlockSpec((1, 1, D_MODEL), lambda i, j: (layer * N_MOD + _mod_row(i, tm), 0, 5))
    return pl.pallas_call(
        functools.partial(_ffn_kernel, tf=tf, d_ff=d_ff),
        grid=(ROWS // tm, pl.cdiv(d_ff, tf)),
        in_specs=[row(D_MODEL), row(D_MODEL), g2, up, up, down, vec, vec],
        out_specs=row(D_MODEL),
        out_shape=jax.ShapeDtypeStruct((ROWS, D_MODEL), F32),
        scratch_shapes=[pltpu.VMEM((tm, D_MODEL), BF16), pltpu.VMEM((tm, D_MODEL), F32)],
        compiler_params=_params(2, 48),
        name="ffn_dense",
    )(h2, x1, mod3, wg, wu, wd, ln_g, ln_b)


def _route_plan(meta, counts):
    i32 = jnp.int32
    e1, e2 = meta[:, 0].astype(i32), meta[:, 1].astype(i32)
    r1, r2 = meta[:, 2].astype(i32), meta[:, 3].astype(i32)
    cnt = counts[0, :N_EXPERTS].astype(i32)
    tiles = (cnt + MOE_TILE - 1) // MOE_TILE
    tile_end = jnp.cumsum(tiles)
    tile_start = tile_end - tiles
    base = tile_start * MOE_TILE
    pos1 = base[e1] + r1
    pos2 = base[e2] + r2
    t = jnp.arange(MOE_TILES, dtype=i32)
    n_used = tile_end[-1]
    used = t < n_used
    owner = jnp.sum((t[:, None] >= tile_end[None, :]).astype(i32), axis=1)
    owner = jnp.minimum(owner, N_EXPERTS - 1)
    rows = jnp.clip(cnt[owner] - (t - tile_start[owner]) * MOE_TILE, 0, MOE_TILE)
    rows = jnp.where(used, rows, 0)
    n_sub = (rows + MOE_SUB - 1) // MOE_SUB
    last_owner = owner[jnp.maximum(n_used - 1, 0)]
    tile_expert = jnp.where(used, owner, last_owner)
    zero_fill = (rows < MOE_TILE).astype(i32)
    return pos1, pos2, tile_expert, n_sub, zero_fill


def _dispatch_kernel(zf_ref, p1_ref, p2_ref, h_ref, xs_ref, zbuf, zsem, sem):
    tm = h_ref.shape[0]

    def zero_copy(k):
        return pltpu.make_async_copy(zbuf, xs_ref.at[pl.ds(k * MOE_TILE, MOE_TILE)], zsem)

    @pl.when(pl.program_id(0) == 0)
    def _():
        zbuf[...] = jnp.zeros_like(zbuf)
        for k in range(MOE_TILES):
            @pl.when(zf_ref[k] != 0)
            def _():
                zero_copy(k).start()
        for k in range(MOE_TILES):
            @pl.when(zf_ref[k] != 0)
            def _():
                zero_copy(k).wait()

    def row_copy(r, p_ref):
        return pltpu.make_async_copy(h_ref.at[pl.ds(r, 1)], xs_ref.at[pl.ds(p_ref[r], 1)], sem)

    def start(r, carry):
        row_copy(r, p1_ref).start()
        row_copy(r, p2_ref).start()
        return carry

    def wait(r, carry):
        row_copy(r, p1_ref).wait()
        row_copy(r, p2_ref).wait()
        return carry

    lax.fori_loop(0, tm, start, 0)
    lax.fori_loop(0, tm, wait, 0)


def _moe_dispatch(h2, pos1, pos2, zero_fill):
    tm = TM_ROUTE
    pos = pl.BlockSpec((tm,), lambda i, zf: (i,), memory_space=pltpu.SMEM)
    return pl.pallas_call(
        _dispatch_kernel,
        grid_spec=pltpu.PrefetchScalarGridSpec(
            num_scalar_prefetch=1,
            grid=(ROWS // tm,),
            in_specs=[pos, pos, pl.BlockSpec((tm, D_MODEL), lambda i, zf: (i, 0))],
            out_specs=pl.BlockSpec(memory_space=pl.ANY),
            scratch_shapes=[pltpu.VMEM((MOE_TILE, D_MODEL), F32),
                            pltpu.SemaphoreType.DMA(()), pltpu.SemaphoreType.DMA(())]),
        out_shape=jax.ShapeDtypeStruct((MOE_ROWS, D_MODEL), F32),
        compiler_params=_params(1, 32),
        name="moe_dispatch",
    )(zero_fill, pos1, pos2, h2)


def _expert_kernel(te_ref, ns_ref, xs_ref, wg_ref, wu_ref, wd_ref, ys_ref, xb):
    i, j = pl.program_id(0), pl.program_id(1)
    n_sub = ns_ref[i]

    @pl.when(j == 0)
    def _():
        xb[...] = xs_ref[...].astype(BF16)
        ys_ref[...] = jnp.zeros_like(ys_ref)

    def accumulate(rows):
        act, down = _swiglu_partial(xb[rows, :], wg_ref[...].astype(BF16), wu_ref[...].astype(BF16),
                                    wd_ref[...].astype(BF16))
        ys_ref[rows, :] += down(act)

    full = MOE_TILE // MOE_SUB

    @pl.when(n_sub == full)
    def _():
        accumulate(slice(None))

    @pl.when((n_sub > 0) & (n_sub < full))
    def _():
        def body(s, carry):
            accumulate(pl.ds(pl.multiple_of(s * MOE_SUB, MOE_SUB), MOE_SUB))
            return carry
        lax.fori_loop(0, n_sub, body, 0)


def _moe_experts(layer, xs, tile_expert, n_sub, wg, wu, wd):
    tf = TF_MOE
    n_j = EXPERT_FF // tf
    w_idx = layer // 2
    ff = lambda i, j, ns: jnp.where(ns[i] > 0, j, n_j - 1)
    up = pl.BlockSpec((None, None, D_MODEL, tf), lambda i, j, te, ns: (w_idx, te[i], 0, ff(i, j, ns)))
    down = pl.BlockSpec((None, None, tf, D_MODEL), lambda i, j, te, ns: (w_idx, te[i], ff(i, j, ns), 0))
    rows = pl.BlockSpec((MOE_TILE, D_MODEL), lambda i, j, te, ns: (i, 0))
    return pl.pallas_call(
        _expert_kernel,
        grid_spec=pltpu.PrefetchScalarGridSpec(
            num_scalar_prefetch=2,
            grid=(MOE_TILES, n_j),
            in_specs=[rows, up, up, down],
            out_specs=rows,
            scratch_shapes=[pltpu.VMEM((MOE_TILE, D_MODEL), BF16)]),
        out_shape=jax.ShapeDtypeStruct((MOE_ROWS, D_MODEL), F32),
        compiler_params=_params(2, 48),
        name="moe_experts",
    )(tile_expert, n_sub, xs, wg, wu, wd)


def _combine_kernel(p1_ref, p2_ref, ys_ref, x1_ref, meta_ref, g2_ref, lng_ref, lnb_ref, o_ref, b1, b2, sem):
    tm = x1_ref.shape[0]

    def row_copy(r, p_ref, buf):
        return pltpu.make_async_copy(ys_ref.at[pl.ds(p_ref[r], 1)], buf.at[pl.ds(r, 1)], sem)

    def start(r, carry):
        row_copy(r, p1_ref, b1).start()
        row_copy(r, p2_ref, b2).start()
        return carry

    def wait(r, carry):
        row_copy(r, p1_ref, b1).wait()
        row_copy(r, p2_ref, b2).wait()
        return carry

    lax.fori_loop(0, tm, start, 0)
    lax.fori_loop(0, tm, wait, 0)
    meta = meta_ref[...]
    f = meta[:, 4:5] * b1[...] + meta[:, 5:6] * b2[...]
    y = DEEPNORM_ALPHA * x1_ref[...] + g2_ref[0] * f
    o_ref[...] = _layer_norm(y) * lng_ref[...] + lnb_ref[...]


def _moe_combine(layer, ys, pos1, pos2, x1, meta, mod3, ln_g, ln_b):
    tm = TM_COMBINE
    pos = pl.BlockSpec((tm,), lambda i: (i,), memory_space=pltpu.SMEM)
    row = lambda w: pl.BlockSpec((tm, w), lambda i: (i, 0))
    vec = pl.BlockSpec((1, D_MODEL), lambda i: (0, 0))
    return pl.pallas_call(
        _combine_kernel,
        grid=(ROWS // tm,),
        in_specs=[pos, pos, pl.BlockSpec(memory_space=pl.ANY), row(D_MODEL), row(LANES),
                  _mod_spec(layer, 5, tm), vec, vec],
        out_specs=row(D_MODEL),
        out_shape=jax.ShapeDtypeStruct((ROWS, D_MODEL), F32),
        scratch_shapes=[pltpu.VMEM((tm, D_MODEL), F32), pltpu.VMEM((tm, D_MODEL), F32),
                        pltpu.SemaphoreType.DMA(())],
        compiler_params=_params(1, 32),
        name="moe_combine",
    )(pos1, pos2, ys, x1, meta, mod3, ln_g, ln_b)


def _rope_tables():
    rows = DEC_SEQ // GRID_W
    row = jnp.repeat(jnp.arange(rows), GRID_W).astype(F32)
    col = jnp.tile(jnp.arange(GRID_W), rows).astype(F32)
    freqs = ROPE_THETA ** (-jnp.arange(ROPE_PAIRS, dtype=F32) / ROPE_PAIRS)
    ang = jnp.stack([row[:, None] * freqs, col[:, None] * freqs], axis=1)
    cos, sin = jnp.cos(ang), jnp.sin(ang)
    cos_h = jnp.concatenate([cos[:, 0], cos[:, 0], cos[:, 1], cos[:, 1]], axis=-1)
    sin_h = jnp.concatenate([-sin[:, 0], sin[:, 0], -sin[:, 1], sin[:, 1]], axis=-1)
    per_tile = LANES // HEAD_DIM
    cos_t = jnp.concatenate([jnp.ones((ROPE_ID_ROWS, LANES), F32), jnp.tile(cos_h, (1, per_tile))], axis=0)
    sin_t = jnp.concatenate([jnp.zeros((ROPE_ID_ROWS, LANES), F32), jnp.tile(sin_h, (1, per_tile))], axis=0)
    return cos_t, sin_t


def kernel(x_prompt, x_sample, c, cache_k, cache_v, state_ssm_re, state_ssm_im, c_ctx, w_ada, b_ada,
           w_in, q_gain, k_gain, ssm_lambda_re, ssm_lambda_im, ssm_log_step, ssm_b_re, ssm_b_im,
           ssm_c_re, ssm_c_im, ssm_d, w_glu, b_glu, w_out, ln1_g, ln1_b, ln2_g, ln2_b,
           ffn_w_gate, ffn_w_up, ffn_w_down, router_w, moe_w_gate, moe_w_up, moe_w_down):
    x = jnp.concatenate([x_prompt.reshape(P_ROWS, D_MODEL), x_sample.reshape(S_ROWS, D_MODEL)], axis=0)
    cvec = jnp.concatenate([c_ctx[None], c, jnp.zeros((N_MOD - 1 - DEC_BATCH, D_MODEL), F32)], axis=0)
    mod3 = _modulation(cvec, w_ada, b_ada).reshape(DEPTH * N_MOD, 1, 6 * D_MODEL)
    rope_cos, rope_sin = _rope_tables()
    lam_bar, b_pad, c_pad = _s5_discretise(ssm_lambda_re, ssm_lambda_im, ssm_log_step,
                                           ssm_b_re, ssm_b_im, ssm_c_re, ssm_c_im)
    cache_k4 = cache_k.reshape(DEC_BATCH, DEPTH, PAST_LEN, KV_WIDTH)
    cache_v4 = cache_v.reshape(DEC_BATCH, DEPTH, PAST_LEN, KV_WIDTH)
    h0_s = jnp.stack([state_ssm_re, state_ssm_im], axis=3)
    h0_s = h0_s.reshape(DEC_BATCH, DEPTH, 4, N_GP, LANES)
    h0_all = jnp.concatenate([jnp.zeros((BATCH, DEPTH, 4, N_GP, LANES), F32), h0_s], axis=0)
    router_pad = jnp.pad(router_w, ((0, 0), (0, 0), (0, LANES - N_EXPERTS)))
    gain2 = lambda g: jnp.tile(g, (1, LANES // HEAD_DIM))

    new_k, new_v, new_re, new_im = [], [], [], []
    for l in range(DEPTH):
        q, k_rot, k_norm, v, u = _in_projection(l, x, mod3, w_in[l], gain2(q_gain[l:l + 1]),
                                                gain2(k_gain[l:l + 1]), rope_cos, rope_sin)
        attn_p, attn_s = _attention(l, q, k_rot, v, cache_k4, cache_v4)
        yf, yb, fin = _s5_scan(u, h0_all[:, l], lam_bar[l], b_pad[l], c_pad[l])
        moe = l % 2 == 1
        outs = _out_projection(l, attn_p, attn_s, yf, yb, u, x, mod3, ssm_d[l:l + 1], w_glu[l], b_glu[l:l + 1],
                               w_out[l], ln1_g[l:l + 1], ln1_b[l:l + 1], router_pad[l // 2] if moe else None)
        if moe:
            x1, h2, meta, counts = outs
            pos1, pos2, tile_expert, n_sub, zero_fill = _route_plan(meta, counts)
            xs = _moe_dispatch(h2, pos1, pos2, zero_fill)
            ys = _moe_experts(l, xs, tile_expert, n_sub, moe_w_gate, moe_w_up, moe_w_down)
            x = _moe_combine(l, ys, pos1, pos2, x1, meta, mod3, ln2_g[l:l + 1], ln2_b[l:l + 1])
        else:
            x1, h2 = outs
            x = _ffn(l, h2, x1, mod3, ffn_w_gate, ffn_w_up, ffn_w_down, ln2_g[l:l + 1], ln2_b[l:l + 1])
        new_k.append(k_norm[:P_ROWS].reshape(BATCH, SEQ, N_KV_HEADS, HEAD_DIM))
        new_v.append(v[:P_ROWS].reshape(BATCH, SEQ, N_KV_HEADS, HEAD_DIM))
        fin_p = fin[:BATCH].reshape(BATCH, 2, 2, SSM_GROUPS, SSM_STATE)
        new_re.append(fin_p[:, :, 0])
        new_im.append(fin_p[:, :, 1])

    y_prompt = x[:P_ROWS].reshape(BATCH, SEQ, D_MODEL)
    y_sample = x[P_ROWS:].reshape(DEC_BATCH, DEC_SEQ, D_MODEL)
    return (y_prompt, y_sample, jnp.stack(new_k, axis=1), jnp.stack(new_v, axis=1),
            jnp.stack(new_re, axis=1), jnp.stack(new_im, axis=1))
```

```python
import functools
import math

import jax
import jax.numpy as jnp
from jax import lax
from jax.experimental import pallas as pl
from jax.experimental.pallas import tpu as pltpu

D_MODEL = 1024
BATCH = 16
SEQ = 256
DEPTH = 4
DEC_BATCH = 2
DEC_SEQ = 2048
PAST_LEN = 256
GRID_W = 64
N_HEADS = 8
N_KV_HEADS = 2
HEAD_DIM = 64
ATTN_WIDTH = N_HEADS * HEAD_DIM
KV_WIDTH = N_KV_HEADS * HEAD_DIM
SSM_WIDTH = D_MODEL - ATTN_WIDTH
SSM_GROUP_CH = 16
SSM_GROUPS = SSM_WIDTH // SSM_GROUP_CH
SSM_STATE = 64
IN_WIDTH = ATTN_WIDTH + 2 * KV_WIDTH + SSM_WIDTH
ROPE_THETA = 10000.0
ROPE_PAIRS = HEAD_DIM // 4
D_FF = 2752
N_EXPERTS = 8
TOP_K = 2
EXPERT_FF = 3584
DEEPNORM_ALPHA = (2 * DEPTH) ** 0.25
NORM_EPS = 1e-6

F32 = jnp.float32
BF16 = jnp.bfloat16
HIGHEST = lax.Precision.HIGHEST
LANES = 128

P_ROWS = BATCH * SEQ
S_ROWS = DEC_BATCH * DEC_SEQ
ROWS = P_ROWS + S_ROWS
N_MOD = 8
GRP = N_HEADS // N_KV_HEADS
SCALE = HEAD_DIM ** -0.5

TM_PROJ = 512
ROPE_ID_ROWS = TM_PROJ
TQ_PROMPT = SEQ
TQ_SAMPLE = 128
SCAN_CHUNK = 256
SCAN_PITCH = SCAN_CHUNK + 4
SCAN_UNROLL = 8
N_GP = SSM_GROUPS // 2
GP_PER_CHUNK = LANES // (2 * SSM_GROUP_CH)
S_CHUNKS = DEC_SEQ // SCAN_CHUNK
TM_FFN = 1024
TF_DENSE = 256
TF_MOE = 512
MOE_TILE = 1024
MOE_SUB = 256
MOE_TILES = TOP_K * ROWS // MOE_TILE + N_EXPERTS
MOE_ROWS = MOE_TILES * MOE_TILE
TM_ROUTE = 1024
TM_COMBINE = 512
META_ROWS = 8
ROW_TILE = 8
DMA_UNROLL = 8
MIB = 1024 * 1024


def _params(n_axes, vmem_mib):
    return pltpu.CompilerParams(dimension_semantics=("arbitrary",) * n_axes,
                                vmem_limit_bytes=vmem_mib * MIB)


def _layer_norm(x):
    mu = jnp.mean(x, axis=-1, keepdims=True)
    xc = x - mu
    var = jnp.mean(xc * xc, axis=-1, keepdims=True)
    return xc * lax.rsqrt(var + NORM_EPS)


def _sigmoid(x):
    return 1.0 / (1.0 + jnp.exp(-x))


def _mod_row(tile, tm):
    p_tiles = P_ROWS // tm
    return jnp.where(tile < p_tiles, 0, 1 + (tile - p_tiles) // (DEC_SEQ // tm))


def _mod_kernel(c_ref, w_ref, b_ref, o_ref):
    c = c_ref[...]
    a = c * _sigmoid(c)
    o_ref[0] = jnp.dot(a, w_ref[0], precision=HIGHEST, preferred_element_type=F32) + b_ref[0]


def _modulation(cvec, w_ada, b_ada):
    tn = 1536
    return pl.pallas_call(
        _mod_kernel,
        grid=(DEPTH, 6 * D_MODEL // tn),
        in_specs=[pl.BlockSpec((N_MOD, D_MODEL), lambda l, j: (0, 0)),
                  pl.BlockSpec((1, D_MODEL, tn), lambda l, j: (l, 0, j)),
                  pl.BlockSpec((1, 1, tn), lambda l, j: (l, 0, j))],
        out_specs=pl.BlockSpec((1, N_MOD, tn), lambda l, j: (l, 0, j)),
        out_shape=jax.ShapeDtypeStruct((DEPTH, N_MOD, 6 * D_MODEL), F32),
        compiler_params=_params(2, 32),
        name="adaln_mod",
    )(cvec, w_ada, b_ada.reshape(DEPTH, 1, 6 * D_MODEL))


def _mod_spec(layer, part, tm):
    return pl.BlockSpec((1, 1, D_MODEL), lambda i: (layer * N_MOD + _mod_row(i, tm), 0, part))


def _group_rows(p_ref, s_ref):
    is_prompt = pl.program_id(0) < P_ROWS // p_ref.shape[0]
    return jnp.where(is_prompt, p_ref[...], s_ref[...])


def _group_specs(tm, width, stacked):
    p_tiles = P_ROWS // tm
    offset = p_tiles if stacked else 0
    return [pl.BlockSpec((tm, width), lambda i: (jnp.minimum(i, p_tiles - 1), 0)),
            pl.BlockSpec((tm, width), lambda i: (jnp.maximum(i - p_tiles, 0) + offset, 0))]


def _inproj_kernel(xp_ref, xs_ref, sh_ref, sc_ref, w_ref, qg_ref, kg_ref, cos_ref, sin_ref,
                   q_ref, kr_ref, kn_ref, v_ref, u_ref, wb_ref):
    @pl.when(pl.program_id(0) == 0)
    def _():
        wb_ref[...] = w_ref[...].astype(BF16)

    h = _layer_norm(_group_rows(xp_ref, xs_ref)) * (1.0 + sc_ref[0]) + sh_ref[0]
    proj = jnp.dot(h.astype(BF16), wb_ref[...], preferred_element_type=F32)

    tm = proj.shape[0]
    head_of = lambda idx: lax.shift_right_logical(idx, int(math.log2(HEAD_DIM)))
    r = head_of(lax.broadcasted_iota(jnp.int32, (LANES, LANES), 0))
    c = head_of(lax.broadcasted_iota(jnp.int32, (LANES, LANES), 1))
    seg = jnp.where(r == c, 1.0 / HEAD_DIM, 0.0).astype(F32)
    lane = lax.broadcasted_iota(jnp.int32, (tm, LANES), 1)
    first_half = (lane & (2 * ROPE_PAIRS - 1)) < ROPE_PAIRS
    cos = cos_ref[...]
    sin = sin_ref[...]

    def head_norm(t, gain):
        ms = jnp.dot(t * t, seg, precision=HIGHEST, preferred_element_type=F32)
        return t * lax.rsqrt(ms + NORM_EPS) * gain

    def rope(t):
        partner = jnp.where(first_half, pltpu.roll(t, LANES - ROPE_PAIRS, 1), pltpu.roll(t, ROPE_PAIRS, 1))
        return t * cos + partner * sin

    for j in range(ATTN_WIDTH // LANES):
        sl = slice(j * LANES, (j + 1) * LANES)
        q_ref[:, sl] = rope(head_norm(proj[:, sl], qg_ref[...])) * SCALE
    k = head_norm(proj[:, ATTN_WIDTH:ATTN_WIDTH + KV_WIDTH], kg_ref[...])
    kn_ref[...] = k
    kr_ref[...] = rope(k)
    v_ref[...] = proj[:, ATTN_WIDTH + KV_WIDTH:ATTN_WIDTH + 2 * KV_WIDTH]
    u_ref[...] = proj[:, ATTN_WIDTH + 2 * KV_WIDTH:]


def _in_projection(layer, x_pair, mod3, w_in, q_gain, k_gain, rope_cos, rope_sin):
    tm = TM_PROJ
    stacked = x_pair[0] is x_pair[1]
    p_tiles = P_ROWS // tm
    t_tiles = DEC_SEQ // tm

    def rope_idx(i):
        return (jnp.where(i < p_tiles, 0, 1 + (i - p_tiles) % t_tiles), 0)

    row = lambda w: pl.BlockSpec((tm, w), lambda i: (i, 0))
    full = lambda a: pl.BlockSpec(a.shape, lambda i: (0,) * a.ndim)
    return pl.pallas_call(
        _inproj_kernel,
        grid=(ROWS // tm,),
        in_specs=_group_specs(tm, D_MODEL, stacked) + [
                  _mod_spec(layer, 0, tm), _mod_spec(layer, 1, tm), full(w_in), full(q_gain), full(k_gain),
                  pl.BlockSpec((tm, LANES), rope_idx), pl.BlockSpec((tm, LANES), rope_idx)],
        out_specs=[row(ATTN_WIDTH), row(KV_WIDTH), row(KV_WIDTH), row(KV_WIDTH), row(SSM_WIDTH)],
        out_shape=[jax.ShapeDtypeStruct((ROWS, ATTN_WIDTH), F32),
                   jax.ShapeDtypeStruct((ROWS, KV_WIDTH), F32),
                   jax.ShapeDtypeStruct((ROWS, KV_WIDTH), F32),
                   jax.ShapeDtypeStruct((ROWS, KV_WIDTH), F32),
                   jax.ShapeDtypeStruct((ROWS, SSM_WIDTH), F32)],
        scratch_shapes=[pltpu.VMEM((D_MODEL, IN_WIDTH), BF16)],
        compiler_params=_params(1, 48),
        name="in_projection",
    )(*x_pair, mod3, mod3, w_in, q_gain, k_gain, rope_cos, rope_sin)


def _attn_kernel(*refs, tq, has_ctx):
    if has_ctx:
        q_ref, k_ref, v_ref, kc_ref, vc_ref, o_ref = refs
    else:
        q_ref, k_ref, v_ref, o_ref = refs
    nt = (((1,), (1,)), ((), ()))
    for kv in range(N_KV_HEADS):
        sl = slice(kv * HEAD_DIM, (kv + 1) * HEAD_DIM)
        k = k_ref[:, sl].astype(BF16)
        v = v_ref[:, sl].astype(BF16)
        heads = [q_ref[:, (kv * GRP + g) * HEAD_DIM:(kv * GRP + g + 1) * HEAD_DIM] for g in range(GRP)]
        qs = jnp.concatenate(heads, axis=0).astype(BF16)
        s = lax.dot_general(qs, k, nt, preferred_element_type=F32)
        m = jnp.max(s, axis=-1, keepdims=True)
        if has_ctx:
            s_ctx = lax.dot_general(qs, kc_ref[:, sl].astype(BF16), nt, preferred_element_type=F32)
            m = jnp.maximum(m, jnp.max(s_ctx, axis=-1, keepdims=True))
        p = jnp.exp(s - m)
        den = jnp.sum(p, axis=-1, keepdims=True)
        o = jnp.dot(p.astype(BF16), v, preferred_element_type=F32)
        if has_ctx:
            p_ctx = jnp.exp(s_ctx - m)
            den = den + jnp.sum(p_ctx, axis=-1, keepdims=True)
            o = o + jnp.dot(p_ctx.astype(BF16), vc_ref[:, sl].astype(BF16), preferred_element_type=F32)
        o = o / den
        for g in range(GRP):
            h = kv * GRP + g
            o_ref[:, h * HEAD_DIM:(h + 1) * HEAD_DIM] = o[g * tq:(g + 1) * tq]


def _attention(layer, q, k_rot, v, cache_k4, cache_v4):
    tq = TQ_PROMPT
    blk = lambda w: pl.BlockSpec((tq, w), lambda b: (b, 0))
    attn_p = pl.pallas_call(
        functools.partial(_attn_kernel, tq=tq, has_ctx=False),
        grid=(BATCH,),
        in_specs=[blk(ATTN_WIDTH), blk(KV_WIDTH), blk(KV_WIDTH)],
        out_specs=blk(ATTN_WIDTH),
        out_shape=jax.ShapeDtypeStruct((P_ROWS, ATTN_WIDTH), F32),
        compiler_params=_params(1, 32),
        name="attn_prompt",
    )(q, k_rot, v)

    tq = TQ_SAMPLE
    q_tiles = DEC_SEQ // tq
    own = pl.BlockSpec((DEC_SEQ, KV_WIDTH), lambda b, j: (P_ROWS // DEC_SEQ + b, 0))
    ctx = pl.BlockSpec((None, None, PAST_LEN, KV_WIDTH), lambda b, j: (b, layer, 0, 0))
    attn_s = pl.pallas_call(
        functools.partial(_attn_kernel, tq=tq, has_ctx=True),
        grid=(DEC_BATCH, q_tiles),
        in_specs=[pl.BlockSpec((tq, ATTN_WIDTH), lambda b, j: (P_ROWS // tq + b * q_tiles + j, 0)),
                  own, own, ctx, ctx],
        out_specs=pl.BlockSpec((tq, ATTN_WIDTH), lambda b, j: (b * q_tiles + j, 0)),
        out_shape=jax.ShapeDtypeStruct((S_ROWS, ATTN_WIDTH), F32),
        compiler_params=_params(2, 48),
        name="attn_sample",
    )(q, k_rot, v, cache_k4, cache_v4)
    return attn_p, attn_s


def _scan_item(i):
    j = i - BATCH
    seq = jnp.where(i < BATCH, i, BATCH + j // S_CHUNKS)
    bwd = jnp.where(i < BATCH, i, BATCH + (j // S_CHUNKS) * S_CHUNKS + (S_CHUNKS - 1) - j % S_CHUNKS)
    return seq, i, bwd


def _s5_kernel(uf_ref, ub_ref, h0_ref, lam_ref, b_ref, c_ref, yf_ref, yb_ref, fin_ref, drive, slab, carry):
    tc, pitch = SCAN_CHUNK, SCAN_PITCH
    i = pl.program_id(0)

    @pl.when((i < BATCH) | ((i - BATCH) % S_CHUNKS == 0))
    def _():
        carry[...] = h0_ref[0]

    def section(d, part):
        return (2 * d + part) * N_GP * pitch

    for d, u_ref in ((0, uf_ref), (1, ub_ref)):
        for ch in range(SSM_WIDTH // LANES):
            u = u_ref[:, ch * LANES:(ch + 1) * LANES].astype(BF16)
            for gp in range(ch * GP_PER_CHUNK, (ch + 1) * GP_PER_CHUNK):
                bu = jnp.dot(u, b_ref[d, gp], preferred_element_type=F32)
                drive[pl.ds(section(d, 0) + gp * pitch, tc), :] = bu[:, :LANES]
                drive[pl.ds(section(d, 1) + gp * pitch, tc), :] = bu[:, LANES:]

    lam = [[lam_ref[d, part] for part in range(2)] for d in range(2)]

    def step(t, h):
        out = []
        for d in range(2):
            tt = t if d == 0 else tc - 1 - t
            rows_re = pl.ds(section(d, 0) + tt, N_GP, stride=pitch)
            rows_im = pl.ds(section(d, 1) + tt, N_GP, stride=pitch)
            ar, ai = lam[d]
            hr, hi = h[2 * d], h[2 * d + 1]
            nr = ar * hr - ai * hi + drive[rows_re, :]
            ni = ar * hi + ai * hr + drive[rows_im, :]
            slab[rows_re, :] = nr
            slab[rows_im, :] = ni
            out += [nr, ni]
        return tuple(out)

    h = lax.fori_loop(0, tc, step, tuple(carry[s] for s in range(4)), unroll=SCAN_UNROLL)
    for s in range(4):
        carry[s] = h[s]
        fin_ref[0, s] = h[s]

    for d, y_ref in ((0, yf_ref), (1, yb_ref)):
        for ch in range(SSM_WIDTH // LANES):
            y = None
            for gp in range(ch * GP_PER_CHUNK, (ch + 1) * GP_PER_CHUNK):
                hs = jnp.concatenate([slab[pl.ds(section(d, 0) + gp * pitch, tc), :],
                                      slab[pl.ds(section(d, 1) + gp * pitch, tc), :]], axis=1).astype(BF16)
                part = jnp.dot(hs, c_ref[d, gp], preferred_element_type=F32)
                y = part if y is None else y + part
            y_ref[:, ch * LANES:(ch + 1) * LANES] = y


def _s5_scan(u, h0, lam_bar, b_pad, c_pad):
    tc = SCAN_CHUNK
    n_items = BATCH + DEC_BATCH * S_CHUNKS
    n_seq = BATCH + DEC_BATCH
    fwd = pl.BlockSpec((tc, SSM_WIDTH), lambda i: (_scan_item(i)[1], 0))
    bwd = pl.BlockSpec((tc, SSM_WIDTH), lambda i: (_scan_item(i)[2], 0))
    state = pl.BlockSpec((1, 4, N_GP, LANES), lambda i: (_scan_item(i)[0], 0, 0, 0))
    full = lambda a: pl.BlockSpec(a.shape, lambda i: (0,) * a.ndim)
    return pl.pallas_call(
        _s5_kernel,
        grid=(n_items,),
        in_specs=[fwd, bwd, state, full(lam_bar), full(b_pad), full(c_pad)],
        out_specs=[fwd, bwd, state],
        out_shape=[jax.ShapeDtypeStruct((ROWS, SSM_WIDTH), F32),
                   jax.ShapeDtypeStruct((ROWS, SSM_WIDTH), F32),
                   jax.ShapeDtypeStruct((n_seq, 4, N_GP, LANES), F32)],
        scratch_shapes=[pltpu.VMEM((4 * N_GP * SCAN_PITCH, LANES), F32),
                        pltpu.VMEM((4 * N_GP * SCAN_PITCH, LANES), F32),
                        pltpu.VMEM((4, N_GP, LANES), F32)],
        compiler_params=_params(1, 48),
        name="s5_scan",
    )(u, u, h0, lam_bar, b_pad, c_pad)


def _s5_discretise(lam_re, lam_im, log_step, b_re, b_im, c_re, c_im):
    delta = jnp.exp(log_step)[..., None]
    mag = jnp.exp(lam_re * delta)
    lbr = mag * jnp.cos(lam_im * delta)
    lbi = mag * jnp.sin(lam_im * delta)
    den = lam_re * lam_re + lam_im * lam_im
    cr = ((lbr - 1.0) * lam_re + lbi * lam_im) / den
    ci = (lbi * lam_re - (lbr - 1.0) * lam_im) / den
    bbr = cr[..., None] * b_re - ci[..., None] * b_im
    bbi = cr[..., None] * b_im + ci[..., None] * b_re
    lam_bar = jnp.stack([lbr, lbi], axis=2).reshape(DEPTH, 2, 2, N_GP, LANES)

    eye2 = jnp.eye(2, dtype=F32)
    slot = jax.nn.one_hot(jnp.arange(N_GP) % GP_PER_CHUNK, GP_PER_CHUNK, dtype=F32)
    bb = jnp.stack([bbr, bbi], axis=0).reshape(2, DEPTH, 2, N_GP, 2, SSM_STATE, SSM_GROUP_CH)
    b_small = jnp.einsum('cldgjph,jk->ldgjhckp', bb, eye2).reshape(DEPTH, 2, N_GP, 2 * SSM_GROUP_CH, 2 * LANES)
    b_pad = jnp.einsum('ldgrc,gs->ldgsrc', b_small, slot).reshape(DEPTH, 2, N_GP, LANES, 2 * LANES)
    cc = jnp.stack([c_re, -c_im], axis=0).reshape(2, DEPTH, 2, N_GP, 2, SSM_GROUP_CH, SSM_STATE)
    c_small = jnp.einsum('cldgjhp,jk->ldgckpjh', cc, eye2).reshape(DEPTH, 2, N_GP, 2 * LANES, 2 * SSM_GROUP_CH)
    c_pad = jnp.einsum('ldgrc,gs->ldgrsc', c_small, slot).reshape(DEPTH, 2, N_GP, 2 * LANES, LANES)
    return lam_bar, b_pad.astype(BF16), c_pad.astype(BF16)


def _outproj_kernel(*refs, moe):
    (attn_p_ref, attn_s_ref, yf_ref, yb_ref, u_ref, xp_ref, xs_ref, g1_ref, sh2_ref, sc2_ref, d_ref, wglu_ref,
     bglu_ref, wout_ref, lng_ref, lnb_ref) = refs[:16]
    if moe:
        router_ref, x1_ref, h2_ref, meta_ref, meta_t_ref, cnt_ref, wglu_b, wout_b, cnt = refs[16:]
    else:
        x1_ref, h2_ref, wglu_b, wout_b = refs[16:]
    attn = _group_rows(attn_p_ref, attn_s_ref)
    x = _group_rows(xp_ref, xs_ref)

    @pl.when(pl.program_id(0) == 0)
    def _():
        wglu_b[...] = wglu_ref[...].astype(BF16)
        wout_b[...] = wout_ref[...].astype(BF16)

    y = yf_ref[...] + yb_ref[...] + d_ref[...] * u_ref[...]
    g = 0.5 * y * (1.0 + jnp.tanh(math.sqrt(2.0 / math.pi) * (y + 0.044715 * (y * y * y))))
    z = jnp.dot(g.astype(BF16), wglu_b[...], preferred_element_type=F32) + bglu_ref[...]
    y_ssm = g * _sigmoid(z)
    mix = (jnp.dot(attn.astype(BF16), wout_b[:ATTN_WIDTH, :], preferred_element_type=F32)
           + jnp.dot(y_ssm.astype(BF16), wout_b[ATTN_WIDTH:, :], preferred_element_type=F32))
    x1 = _layer_norm(DEEPNORM_ALPHA * x + g1_ref[0] * mix) * lng_ref[...] + lnb_ref[...]
    x1_ref[...] = x1
    h2 = _layer_norm(x1) * (1.0 + sc2_ref[0]) + sh2_ref[0]
    h2_ref[...] = h2

    if moe:
        @pl.when(pl.program_id(0) == 0)
        def _():
            cnt[...] = jnp.zeros_like(cnt)

        router = router_ref[...]
        h_hi, r_hi = h2.astype(BF16), router.astype(BF16)
        h_lo = (h2 - h_hi.astype(F32)).astype(BF16)
        r_lo = (router - r_hi.astype(F32)).astype(BF16)
        logits = (jnp.dot(h_hi, r_hi, preferred_element_type=F32)
                  + (jnp.dot(h_hi, r_lo, preferred_element_type=F32)
                     + jnp.dot(h_lo, r_hi, preferred_element_type=F32)))
        tm = logits.shape[0]
        lane = lax.broadcasted_iota(jnp.int32, logits.shape, 1).astype(F32)
        neg = jnp.float32(-jnp.inf)
        l1 = jnp.where(lane < N_EXPERTS, logits, neg)
        m1 = jnp.max(l1, axis=-1, keepdims=True)
        i1 = jnp.min(jnp.where(l1 == m1, lane, float(LANES)), axis=-1, keepdims=True)
        l2 = jnp.where(lane == i1, neg, l1)
        m2 = jnp.max(l2, axis=-1, keepdims=True)
        i2 = jnp.min(jnp.where(l2 == m2, lane, float(LANES)), axis=-1, keepdims=True)
        e2 = jnp.exp(m2 - m1)
        w1 = 1.0 / (1.0 + e2)
        w2 = e2 / (1.0 + e2)
        hit = jnp.where((lane == i1) | (lane == i2), 1.0, 0.0)
        r = lax.broadcasted_iota(jnp.int32, (tm, tm), 0)
        c = lax.broadcasted_iota(jnp.int32, (tm, tm), 1)
        earlier = jnp.where(c < r, 1.0, 0.0).astype(BF16)
        rank = cnt[...] + jnp.dot(earlier, hit.astype(BF16), preferred_element_type=F32)
        r1 = jnp.sum(jnp.where(lane == i1, rank, 0.0), axis=-1, keepdims=True)
        r2 = jnp.sum(jnp.where(lane == i2, rank, 0.0), axis=-1, keepdims=True)
        cnt[...] = cnt[...] + jnp.sum(hit, axis=0, keepdims=True)
        cnt_ref[...] = cnt[...]
        fields = (i1, i2, r1, r2, w1, w2)
        meta = jnp.zeros_like(logits)
        for k, f in enumerate(fields):
            meta = jnp.where(lane == float(k), f, meta)
        meta_ref[...] = meta
        meta_t_ref[...] = meta.T[:meta_t_ref.shape[0], :]


def _out_projection(layer, attn_p, attn_s, yf, yb, u, x_pair, mod3, d_skip, w_glu, b_glu, w_out, ln_g, ln_b,
                    router):
    tm = TM_PROJ
    moe = router is not None
    row = lambda w: pl.BlockSpec((tm, w), lambda i: (i, 0))
    full = lambda a: pl.BlockSpec(a.shape, lambda i: (0,) * a.ndim)
    args = [attn_p, attn_s, yf, yb, u, *x_pair, mod3, mod3, mod3, d_skip, w_glu, b_glu, w_out, ln_g, ln_b]
    in_specs = (_group_specs(tm, ATTN_WIDTH, False)
                + [row(SSM_WIDTH), row(SSM_WIDTH), row(SSM_WIDTH)]
                + _group_specs(tm, D_MODEL, x_pair[0] is x_pair[1])
                + [_mod_spec(layer, 2, tm), _mod_spec(layer, 3, tm), _mod_spec(layer, 4, tm),
                   full(d_skip), full(w_glu), full(b_glu), full(w_out), full(ln_g), full(ln_b)])
    out_specs = [row(D_MODEL), row(D_MODEL)]
    out_shape = [jax.ShapeDtypeStruct((ROWS, D_MODEL), F32), jax.ShapeDtypeStruct((ROWS, D_MODEL), F32)]
    scratch = [pltpu.VMEM((SSM_WIDTH, SSM_WIDTH), BF16), pltpu.VMEM((D_MODEL, D_MODEL), BF16)]
    if moe:
        args.append(router)
        in_specs.append(full(router))
        out_specs += [row(LANES), pl.BlockSpec((META_ROWS, tm), lambda i: (0, i)),
                      pl.BlockSpec((1, LANES), lambda i: (0, 0))]
        out_shape += [jax.ShapeDtypeStruct((ROWS, LANES), F32), jax.ShapeDtypeStruct((META_ROWS, ROWS), F32),
                      jax.ShapeDtypeStruct((1, LANES), F32)]
        scratch.append(pltpu.VMEM((1, LANES), F32))
    return pl.pallas_call(
        functools.partial(_outproj_kernel, moe=moe),
        grid=(ROWS // tm,),
        in_specs=in_specs,
        out_specs=out_specs,
        out_shape=out_shape,
        scratch_shapes=scratch,
        compiler_params=_params(1, 48),
        name="out_projection",
    )(*args)


def _swiglu_partial(xb, wg, wu, wd):
    a = jnp.dot(xb, wg, preferred_element_type=F32)
    b = jnp.dot(xb, wu, preferred_element_type=F32)
    act = a * _sigmoid(a) * b
    return act, lambda act_: jnp.dot(act_.astype(BF16), wd, preferred_element_type=F32)


def _ffn_kernel(h_ref, x1_ref, g2_ref, wg_ref, wu_ref, wd_ref, lng_ref, lnb_ref, o_ref, hb, acc, *, tf, d_ff):
    j = pl.program_id(1)

    @pl.when(j == 0)
    def _():
        hb[...] = h_ref[...].astype(BF16)
        acc[...] = jnp.zeros_like(acc)

    wd = wd_ref[...]
    if d_ff % tf:
        r = j * tf + lax.broadcasted_iota(jnp.int32, wd.shape, 0)
        wd = jnp.where(r < d_ff, wd, 0.0)
    act, down = _swiglu_partial(hb[...], wg_ref[...].astype(BF16), wu_ref[...].astype(BF16), wd.astype(BF16))
    if d_ff % tf:
        col = j * tf + lax.broadcasted_iota(jnp.int32, act.shape, 1)
        act = jnp.where(col < d_ff, act, 0.0)
    acc[...] += down(act)

    @pl.when(j == pl.num_programs(1) - 1)
    def _():
        y = DEEPNORM_ALPHA * x1_ref[...] + g2_ref[0] * acc[...]
        o_ref[...] = _layer_norm(y) * lng_ref[...] + lnb_ref[...]


def _ffn(layer, h2, x1, mod3, wg, wu, wd, ln_g, ln_b):
    tm, tf, d_ff = TM_FFN, TF_DENSE, D_FF
    w_idx = layer // 2
    up = pl.BlockSpec((None, D_MODEL, tf), lambda i, j: (w_idx, 0, j))
    down = pl.BlockSpec((None, tf, D_MODEL), lambda i, j: (w_idx, j, 0))
    row = lambda w: pl.BlockSpec((tm, w), lambda i, j: (i, 0))
    vec = pl.BlockSpec((1, D_MODEL), lambda i, j: (0, 0))
    g2 = pl.BlockSpec((1, 1, D_MODEL), lambda i, j: (layer * N_MOD + _mod_row(i, tm), 0, 5))
    return pl.pallas_call(
        functools.partial(_ffn_kernel, tf=tf, d_ff=d_ff),
        grid=(ROWS // tm, pl.cdiv(d_ff, tf)),
        in_specs=[row(D_MODEL), row(D_MODEL), g2, up, up, down, vec, vec],
        out_specs=row(D_MODEL),
        out_shape=jax.ShapeDtypeStruct((ROWS, D_MODEL), F32),
        scratch_shapes=[pltpu.VMEM((tm, D_MODEL), BF16), pltpu.VMEM((tm, D_MODEL), F32)],
        compiler_params=_params(2, 48),
        name="ffn_dense",
    )(h2, x1, mod3, wg, wu, wd, ln_g, ln_b)


def _route_plan(meta_t, counts):
    i32 = jnp.int32
    e1, e2 = meta_t[0].astype(i32), meta_t[1].astype(i32)
    r1, r2 = meta_t[2].astype(i32), meta_t[3].astype(i32)
    cnt = counts[0, :N_EXPERTS].astype(i32)
    tiles = (cnt + MOE_TILE - 1) // MOE_TILE
    tile_end = jnp.cumsum(tiles)
    tile_start = tile_end - tiles
    base = tile_start * MOE_TILE
    pos1 = (base[e1] + r1) * ROW_TILE
    pos2 = (base[e2] + r2) * ROW_TILE
    t = jnp.arange(MOE_TILES, dtype=i32)
    n_used = tile_end[-1]
    used = t < n_used
    owner = jnp.sum((t[:, None] >= tile_end[None, :]).astype(i32), axis=1)
    owner = jnp.minimum(owner, N_EXPERTS - 1)
    rows = jnp.clip(cnt[owner] - (t - tile_start[owner]) * MOE_TILE, 0, MOE_TILE)
    rows = jnp.where(used, rows, 0)
    n_sub = (rows + MOE_SUB - 1) // MOE_SUB
    last_owner = owner[jnp.maximum(n_used - 1, 0)]
    tile_expert = jnp.where(used, owner, last_owner)
    zero_fill = (rows < MOE_TILE).astype(i32)
    return pos1, pos2, tile_expert, n_sub, zero_fill


def _slot_rows(start):
    return pl.ds(pl.multiple_of(start, ROW_TILE), ROW_TILE)


def _dispatch_kernel(zf_ref, p1_ref, p2_ref, h_ref, xs_ref, src, zbuf, zsem, sem):
    tm = h_ref.shape[0]
    tile_rows = MOE_TILE * ROW_TILE

    def zero_copy(k):
        return pltpu.make_async_copy(zbuf, xs_ref.at[pl.ds(k * tile_rows, tile_rows)], zsem)

    @pl.when(pl.program_id(0) == 0)
    def _():
        zbuf[...] = jnp.zeros_like(zbuf)
        for k in range(MOE_TILES):
            @pl.when(zf_ref[k] != 0)
            def _():
                zero_copy(k).start()
        for k in range(MOE_TILES):
            @pl.when(zf_ref[k] != 0)
            def _():
                zero_copy(k).wait()

    for s in range(ROW_TILE):
        src[pl.ds(s, tm, stride=ROW_TILE), :] = h_ref[:, s * LANES:(s + 1) * LANES]

    def row_copy(r, p_ref):
        return pltpu.make_async_copy(src.at[_slot_rows(r * ROW_TILE)], xs_ref.at[_slot_rows(p_ref[r])], sem)

    def start(r, carry):
        row_copy(r, p1_ref).start()
        row_copy(r, p2_ref).start()
        return carry

    def wait(r, carry):
        row_copy(r, p1_ref).wait()
        row_copy(r, p2_ref).wait()
        return carry

    lax.fori_loop(0, tm, start, 0, unroll=DMA_UNROLL)
    lax.fori_loop(0, tm, wait, 0, unroll=DMA_UNROLL)


def _moe_dispatch(h2, pos1, pos2, zero_fill):
    tm = TM_ROUTE
    pos = pl.BlockSpec((tm,), lambda i, zf: (i,), memory_space=pltpu.SMEM)
    return pl.pallas_call(
        _dispatch_kernel,
        grid_spec=pltpu.PrefetchScalarGridSpec(
            num_scalar_prefetch=1,
            grid=(ROWS // tm,),
            in_specs=[pos, pos, pl.BlockSpec((tm, D_MODEL), lambda i, zf: (i, 0))],
            out_specs=pl.BlockSpec(memory_space=pl.ANY),
            scratch_shapes=[pltpu.VMEM((tm * ROW_TILE, LANES), F32),
                            pltpu.VMEM((MOE_TILE * ROW_TILE, LANES), F32),
                            pltpu.SemaphoreType.DMA(()), pltpu.SemaphoreType.DMA(())]),
        out_shape=jax.ShapeDtypeStruct((MOE_ROWS * ROW_TILE, LANES), F32),
        compiler_params=_params(1, 32),
        name="moe_dispatch",
    )(zero_fill, pos1, pos2, h2)


def _expert_kernel(te_ref, ns_ref, xs_ref, wg_ref, wu_ref, wd_ref, ys_ref, xb, acc):
    i, j = pl.program_id(0), pl.program_id(1)
    n_sub = ns_ref[i]

    @pl.when(j == 0)
    def _():
        for s in range(ROW_TILE):
            xb[:, s * LANES:(s + 1) * LANES] = xs_ref[pl.ds(s, MOE_TILE, stride=ROW_TILE), :].astype(BF16)
        acc[...] = jnp.zeros_like(acc)

    def accumulate(rows):
        act, down = _swiglu_partial(xb[rows, :], wg_ref[...].astype(BF16), wu_ref[...].astype(BF16),
                                    wd_ref[...].astype(BF16))
        acc[rows, :] += down(act)

    full = MOE_TILE // MOE_SUB

    @pl.when(n_sub == full)
    def _():
        accumulate(slice(None))

    @pl.when((n_sub > 0) & (n_sub < full))
    def _():
        def body(s, carry):
            accumulate(pl.ds(pl.multiple_of(s * MOE_SUB, MOE_SUB), MOE_SUB))
            return carry
        lax.fori_loop(0, n_sub, body, 0)

    @pl.when(j == pl.num_programs(1) - 1)
    def _():
        for s in range(ROW_TILE):
            ys_ref[pl.ds(s, MOE_TILE, stride=ROW_TILE), :] = acc[:, s * LANES:(s + 1) * LANES]


def _moe_experts(layer, xs, tile_expert, n_sub, wg, wu, wd):
    tf = TF_MOE
    n_j = EXPERT_FF // tf
    w_idx = layer // 2
    ff = lambda i, j, ns: jnp.where(ns[i] > 0, j, n_j - 1)
    up = pl.BlockSpec((None, None, D_MODEL, tf), lambda i, j, te, ns: (w_idx, te[i], 0, ff(i, j, ns)))
    down = pl.BlockSpec((None, None, tf, D_MODEL), lambda i, j, te, ns: (w_idx, te[i], ff(i, j, ns), 0))
    rows = pl.BlockSpec((MOE_TILE * ROW_TILE, LANES), lambda i, j, te, ns: (i, 0))
    return pl.pallas_call(
        _expert_kernel,
        grid_spec=pltpu.PrefetchScalarGridSpec(
            num_scalar_prefetch=2,
            grid=(MOE_TILES, n_j),
            in_specs=[rows, up, up, down],
            out_specs=rows,
            scratch_shapes=[pltpu.VMEM((MOE_TILE, D_MODEL), BF16), pltpu.VMEM((MOE_TILE, D_MODEL), F32)]),
        out_shape=jax.ShapeDtypeStruct((MOE_ROWS * ROW_TILE, LANES), F32),
        compiler_params=_params(2, 48),
        name="moe_experts",
    )(tile_expert, n_sub, xs, wg, wu, wd)


def _combine_kernel(p1_ref, p2_ref, ys_ref, x1_ref, meta_ref, g2_ref, lng_ref, lnb_ref, o_ref, b1, b2, f, sem):
    tm = x1_ref.shape[0]

    def row_copy(r, p_ref, buf):
        return pltpu.make_async_copy(ys_ref.at[_slot_rows(p_ref[r])], buf.at[_slot_rows(r * ROW_TILE)], sem)

    def start(r, carry):
        row_copy(r, p1_ref, b1).start()
        row_copy(r, p2_ref, b2).start()
        return carry

    def wait(r, carry):
        row_copy(r, p1_ref, b1).wait()
        row_copy(r, p2_ref, b2).wait()
        return carry

    lax.fori_loop(0, tm, start, 0, unroll=DMA_UNROLL)
    lax.fori_loop(0, tm, wait, 0, unroll=DMA_UNROLL)
    meta = meta_ref[...]
    w1, w2 = meta[:, 4:5], meta[:, 5:6]
    for s in range(ROW_TILE):
        rows = pl.ds(s, tm, stride=ROW_TILE)
        f[:, s * LANES:(s + 1) * LANES] = w1 * b1[rows, :] + w2 * b2[rows, :]
    y = DEEPNORM_ALPHA * x1_ref[...] + g2_ref[0] * f[...]
    o_ref[...] = _layer_norm(y) * lng_ref[...] + lnb_ref[...]


def _moe_combine(layer, ys, pos1, pos2, x1, meta, mod3, ln_g, ln_b):
    tm = TM_COMBINE
    pos = pl.BlockSpec((tm,), lambda i: (i,), memory_space=pltpu.SMEM)
    row = lambda w: pl.BlockSpec((tm, w), lambda i: (i, 0))
    vec = pl.BlockSpec((1, D_MODEL), lambda i: (0, 0))
    return pl.pallas_call(
        _combine_kernel,
        grid=(ROWS // tm,),
        in_specs=[pos, pos, pl.BlockSpec(memory_space=pl.ANY), row(D_MODEL), row(LANES),
                  _mod_spec(layer, 5, tm), vec, vec],
        out_specs=row(D_MODEL),
        out_shape=jax.ShapeDtypeStruct((ROWS, D_MODEL), F32),
        scratch_shapes=[pltpu.VMEM((tm * ROW_TILE, LANES), F32), pltpu.VMEM((tm * ROW_TILE, LANES), F32),
                        pltpu.VMEM((tm, D_MODEL), F32), pltpu.SemaphoreType.DMA(())],
        compiler_params=_params(1, 32),
        name="moe_combine",
    )(pos1, pos2, ys, x1, meta, mod3, ln_g, ln_b)


def _rope_tables():
    rows = DEC_SEQ // GRID_W
    row = jnp.repeat(jnp.arange(rows), GRID_W).astype(F32)
    col = jnp.tile(jnp.arange(GRID_W), rows).astype(F32)
    freqs = ROPE_THETA ** (-jnp.arange(ROPE_PAIRS, dtype=F32) / ROPE_PAIRS)
    ang = jnp.stack([row[:, None] * freqs, col[:, None] * freqs], axis=1)
    cos, sin = jnp.cos(ang), jnp.sin(ang)
    cos_h = jnp.concatenate([cos[:, 0], cos[:, 0], cos[:, 1], cos[:, 1]], axis=-1)
    sin_h = jnp.concatenate([-sin[:, 0], sin[:, 0], -sin[:, 1], sin[:, 1]], axis=-1)
    per_tile = LANES // HEAD_DIM
    cos_t = jnp.concatenate([jnp.ones((ROPE_ID_ROWS, LANES), F32), jnp.tile(cos_h, (1, per_tile))], axis=0)
    sin_t = jnp.concatenate([jnp.zeros((ROPE_ID_ROWS, LANES), F32), jnp.tile(sin_h, (1, per_tile))], axis=0)
    return cos_t, sin_t


def kernel(x_prompt, x_sample, c, cache_k, cache_v, state_ssm_re, state_ssm_im, c_ctx, w_ada, b_ada,
           w_in, q_gain, k_gain, ssm_lambda_re, ssm_lambda_im, ssm_log_step, ssm_b_re, ssm_b_im,
           ssm_c_re, ssm_c_im, ssm_d, w_glu, b_glu, w_out, ln1_g, ln1_b, ln2_g, ln2_b,
           ffn_w_gate, ffn_w_up, ffn_w_down, router_w, moe_w_gate, moe_w_up, moe_w_down):
    x_pair = (x_prompt.reshape(P_ROWS, D_MODEL), x_sample.reshape(S_ROWS, D_MODEL))
    cvec = jnp.concatenate([c_ctx[None], c, jnp.zeros((N_MOD - 1 - DEC_BATCH, D_MODEL), F32)], axis=0)
    mod3 = _modulation(cvec, w_ada, b_ada).reshape(DEPTH * N_MOD, 1, 6 * D_MODEL)
    rope_cos, rope_sin = _rope_tables()
    lam_bar, b_pad, c_pad = _s5_discretise(ssm_lambda_re, ssm_lambda_im, ssm_log_step,
                                           ssm_b_re, ssm_b_im, ssm_c_re, ssm_c_im)
    cache_k4 = cache_k.reshape(DEC_BATCH, DEPTH, PAST_LEN, KV_WIDTH)
    cache_v4 = cache_v.reshape(DEC_BATCH, DEPTH, PAST_LEN, KV_WIDTH)
    h0_s = jnp.stack([state_ssm_re, state_ssm_im], axis=3)
    h0_s = h0_s.reshape(DEC_BATCH, DEPTH, 4, N_GP, LANES)
    h0_all = jnp.concatenate([jnp.zeros((BATCH, DEPTH, 4, N_GP, LANES), F32), h0_s], axis=0)
    router_pad = jnp.pad(router_w, ((0, 0), (0, 0), (0, LANES - N_EXPERTS)))
    gain2 = lambda g: jnp.tile(g, (1, LANES // HEAD_DIM))

    new_k, new_v, new_re, new_im = [], [], [], []
    for l in range(DEPTH):
        q, k_rot, k_norm, v, u = _in_projection(l, x_pair, mod3, w_in[l], gain2(q_gain[l:l + 1]),
                                                gain2(k_gain[l:l + 1]), rope_cos, rope_sin)
        attn_p, attn_s = _attention(l, q, k_rot, v, cache_k4, cache_v4)
        yf, yb, fin = _s5_scan(u, h0_all[:, l], lam_bar[l], b_pad[l], c_pad[l])
        moe = l % 2 == 1
        outs = _out_projection(l, attn_p, attn_s, yf, yb, u, x_pair, mod3, ssm_d[l:l + 1], w_glu[l],
                               b_glu[l:l + 1], w_out[l], ln1_g[l:l + 1], ln1_b[l:l + 1],
                               router_pad[l // 2] if moe else None)
        if moe:
            x1, h2, meta, meta_t, counts = outs
            pos1, pos2, tile_expert, n_sub, zero_fill = _route_plan(meta_t, counts)
            xs = _moe_dispatch(h2, pos1, pos2, zero_fill)
            ys = _moe_experts(l, xs, tile_expert, n_sub, moe_w_gate, moe_w_up, moe_w_down)
            x = _moe_combine(l, ys, pos1, pos2, x1, meta, mod3, ln2_g[l:l + 1], ln2_b[l:l + 1])
        else:
            x1, h2 = outs
            x = _ffn(l, h2, x1, mod3, ffn_w_gate, ffn_w_up, ffn_w_down, ln2_g[l:l + 1], ln2_b[l:l + 1])
        x_pair = (x, x)
        new_k.append(k_norm[:P_ROWS].reshape(BATCH, SEQ, N_KV_HEADS, HEAD_DIM))
        new_v.append(v[:P_ROWS].reshape(BATCH, SEQ, N_KV_HEADS, HEAD_DIM))
        fin_p = fin[:BATCH].reshape(BATCH, 2, 2, SSM_GROUPS, SSM_STATE)
        new_re.append(fin_p[:, :, 0])
        new_im.append(fin_p[:, :, 1])

    y_prompt = x[:P_ROWS].reshape(BATCH, SEQ, D_MODEL)
    y_sample = x[P_ROWS:].reshape(DEC_BATCH, DEC_SEQ, D_MODEL)
    return (y_prompt, y_sample, jnp.stack(new_k, axis=1), jnp.stack(new_v, axis=1),
            jnp.stack(new_re, axis=1), jnp.stack(new_im, axis=1))
```

```python
import functools
import math

import jax
import jax.numpy as jnp
from jax import lax
from jax.experimental import pallas as pl
from jax.experimental.pallas import tpu as pltpu

D_MODEL = 1024
BATCH = 16
SEQ = 256
DEPTH = 4
DEC_BATCH = 2
DEC_SEQ = 2048
PAST_LEN = 256
GRID_W = 64
N_HEADS = 8
N_KV_HEADS = 2
HEAD_DIM = 64
ATTN_WIDTH = N_HEADS * HEAD_DIM
KV_WIDTH = N_KV_HEADS * HEAD_DIM
SSM_WIDTH = D_MODEL - ATTN_WIDTH
SSM_GROUP_CH = 16
SSM_GROUPS = SSM_WIDTH // SSM_GROUP_CH
SSM_STATE = 64
IN_WIDTH = ATTN_WIDTH + 2 * KV_WIDTH + SSM_WIDTH
ROPE_THETA = 10000.0
ROPE_PAIRS = HEAD_DIM // 4
D_FF = 2752
N_EXPERTS = 8
TOP_K = 2
EXPERT_FF = 3584
DEEPNORM_ALPHA = (2 * DEPTH) ** 0.25
NORM_EPS = 1e-6

F32 = jnp.float32
BF16 = jnp.bfloat16
HIGHEST = lax.Precision.HIGHEST
LANES = 128

P_ROWS = BATCH * SEQ
S_ROWS = DEC_BATCH * DEC_SEQ
ROWS = P_ROWS + S_ROWS
N_MOD = 8
GRP = N_HEADS // N_KV_HEADS
SCALE = HEAD_DIM ** -0.5

TM_PROJ = 512
ROPE_ID_ROWS = TM_PROJ
TQ_PROMPT = SEQ
TQ_SAMPLE = 128
SCAN_CHUNK = 256
SCAN_PITCH = SCAN_CHUNK + 4
SCAN_UNROLL = 8
N_GP = SSM_GROUPS // 2
GP_PER_CHUNK = LANES // (2 * SSM_GROUP_CH)
S_CHUNKS = DEC_SEQ // SCAN_CHUNK
TM_FFN = 1024
TF_DENSE = 512
TF_MOE = 512
MOE_TILE = 1024
MOE_SUB = 256
MOE_TILES = TOP_K * ROWS // MOE_TILE + N_EXPERTS
MOE_ROWS = MOE_TILES * MOE_TILE
TM_ROUTE = 1024
TM_COMBINE = 512
META_ROWS = 8
ROW_TILE = 8
DMA_UNROLL = 8
MIB = 1024 * 1024


def _params(n_axes, vmem_mib):
    return pltpu.CompilerParams(dimension_semantics=("arbitrary",) * n_axes,
                                vmem_limit_bytes=vmem_mib * MIB)


def _layer_norm(x):
    mu = jnp.mean(x, axis=-1, keepdims=True)
    xc = x - mu
    var = jnp.mean(xc * xc, axis=-1, keepdims=True)
    return xc * lax.rsqrt(var + NORM_EPS)


def _sigmoid(x):
    return 1.0 / (1.0 + jnp.exp(-x))


def _mod_row(tile, tm):
    p_tiles = P_ROWS // tm
    return jnp.where(tile < p_tiles, 0, 1 + (tile - p_tiles) // (DEC_SEQ // tm))


def _mod_kernel(c_ref, w_ref, b_ref, o_ref):
    c = c_ref[...]
    a = c * _sigmoid(c)
    o_ref[0] = jnp.dot(a, w_ref[0], precision=HIGHEST, preferred_element_type=F32) + b_ref[0]


def _modulation(cvec, w_ada, b_ada):
    tn = 1536
    return pl.pallas_call(
        _mod_kernel,
        grid=(DEPTH, 6 * D_MODEL // tn),
        in_specs=[pl.BlockSpec((N_MOD, D_MODEL), lambda l, j: (0, 0)),
                  pl.BlockSpec((1, D_MODEL, tn), lambda l, j: (l, 0, j)),
                  pl.BlockSpec((1, 1, tn), lambda l, j: (l, 0, j))],
        out_specs=pl.BlockSpec((1, N_MOD, tn), lambda l, j: (l, 0, j)),
        out_shape=jax.ShapeDtypeStruct((DEPTH, N_MOD, 6 * D_MODEL), F32),
        compiler_params=_params(2, 32),
        name="adaln_mod",
    )(cvec, w_ada, b_ada.reshape(DEPTH, 1, 6 * D_MODEL))


def _layer_spec(stacked, index):
    zeros = (0,) * (stacked.ndim - 1)
    return pl.BlockSpec((None,) + stacked.shape[1:], lambda i: (index,) + zeros)


def _mod_spec(layer, part, tm):
    return pl.BlockSpec((1, 1, D_MODEL), lambda i: (layer * N_MOD + _mod_row(i, tm), 0, part))


def _group_rows(p_ref, s_ref):
    is_prompt = pl.program_id(0) < P_ROWS // p_ref.shape[0]
    return jnp.where(is_prompt, p_ref[...], s_ref[...])


def _group_specs(tm, width, stacked):
    p_tiles = P_ROWS // tm
    offset = p_tiles if stacked else 0
    return [pl.BlockSpec((tm, width), lambda i: (jnp.minimum(i, p_tiles - 1), 0)),
            pl.BlockSpec((tm, width), lambda i: (jnp.maximum(i - p_tiles, 0) + offset, 0))]


def _inproj_kernel(xp_ref, xs_ref, sh_ref, sc_ref, w_ref, qg_ref, kg_ref, cos_ref, sin_ref,
                   q_ref, kr_ref, kn_ref, v_ref, u_ref, wb_ref):
    @pl.when(pl.program_id(0) == 0)
    def _():
        wb_ref[...] = w_ref[...].astype(BF16)

    h = _layer_norm(_group_rows(xp_ref, xs_ref)) * (1.0 + sc_ref[0]) + sh_ref[0]
    proj = jnp.dot(h.astype(BF16), wb_ref[...], preferred_element_type=F32)

    tm = proj.shape[0]
    head_of = lambda idx: lax.shift_right_logical(idx, int(math.log2(HEAD_DIM)))
    r = head_of(lax.broadcasted_iota(jnp.int32, (LANES, LANES), 0))
    c = head_of(lax.broadcasted_iota(jnp.int32, (LANES, LANES), 1))
    seg = jnp.where(r == c, 1.0 / HEAD_DIM, 0.0).astype(BF16)
    lane = lax.broadcasted_iota(jnp.int32, (tm, LANES), 1)
    first_half = (lane & (2 * ROPE_PAIRS - 1)) < ROPE_PAIRS
    cos = cos_ref[...]
    sin = sin_ref[...]

    def head_norm(t, gain):
        sq = t * t
        hi = sq.astype(BF16)
        lo = (sq - hi.astype(F32)).astype(BF16)
        ms = (jnp.dot(hi, seg, preferred_element_type=F32) + jnp.dot(lo, seg, preferred_element_type=F32))
        return t * lax.rsqrt(ms + NORM_EPS) * gain

    def rope(t):
        partner = jnp.where(first_half, pltpu.roll(t, LANES - ROPE_PAIRS, 1), pltpu.roll(t, ROPE_PAIRS, 1))
        return t * cos + partner * sin

    for j in range(ATTN_WIDTH // LANES):
        sl = slice(j * LANES, (j + 1) * LANES)
        q_ref[:, sl] = rope(head_norm(proj[:, sl], qg_ref[...])) * SCALE
    k = head_norm(proj[:, ATTN_WIDTH:ATTN_WIDTH + KV_WIDTH], kg_ref[...])
    kn_ref[...] = k
    kr_ref[...] = rope(k)
    v_ref[...] = proj[:, ATTN_WIDTH + KV_WIDTH:ATTN_WIDTH + 2 * KV_WIDTH]
    u_ref[...] = proj[:, ATTN_WIDTH + 2 * KV_WIDTH:]


def _in_projection(layer, x_pair, mod3, w_in, q_gain, k_gain, rope_cos, rope_sin):
    tm = TM_PROJ
    stacked = x_pair[0] is x_pair[1]
    p_tiles = P_ROWS // tm
    t_tiles = DEC_SEQ // tm

    def rope_idx(i):
        return (jnp.where(i < p_tiles, 0, 1 + (i - p_tiles) % t_tiles), 0)

    row = lambda w: pl.BlockSpec((tm, w), lambda i: (i, 0))
    full = lambda a: pl.BlockSpec(a.shape, lambda i: (0,) * a.ndim)
    return pl.pallas_call(
        _inproj_kernel,
        grid=(ROWS // tm,),
        in_specs=_group_specs(tm, D_MODEL, stacked) + [
                  _mod_spec(layer, 0, tm), _mod_spec(layer, 1, tm), _layer_spec(w_in, layer),
                  full(q_gain), full(k_gain),
                  pl.BlockSpec((tm, LANES), rope_idx), pl.BlockSpec((tm, LANES), rope_idx)],
        out_specs=[row(ATTN_WIDTH), row(KV_WIDTH), row(KV_WIDTH), row(KV_WIDTH), row(SSM_WIDTH)],
        out_shape=[jax.ShapeDtypeStruct((ROWS, ATTN_WIDTH), F32),
                   jax.ShapeDtypeStruct((ROWS, KV_WIDTH), F32),
                   jax.ShapeDtypeStruct((ROWS, KV_WIDTH), F32),
                   jax.ShapeDtypeStruct((ROWS, KV_WIDTH), F32),
                   jax.ShapeDtypeStruct((ROWS, SSM_WIDTH), F32)],
        scratch_shapes=[pltpu.VMEM((D_MODEL, IN_WIDTH), BF16)],
        compiler_params=_params(1, 48),
        name="in_projection",
    )(*x_pair, mod3, mod3, w_in, q_gain, k_gain, rope_cos, rope_sin)


def _attn_kernel(*refs, tq, has_ctx):
    if has_ctx:
        q_ref, k_ref, v_ref, kc_ref, vc_ref, o_ref = refs
    else:
        q_ref, k_ref, v_ref, o_ref = refs
    nt = (((1,), (1,)), ((), ()))
    for kv in range(N_KV_HEADS):
        sl = slice(kv * HEAD_DIM, (kv + 1) * HEAD_DIM)
        k = k_ref[:, sl].astype(BF16)
        v = v_ref[:, sl].astype(BF16)
        heads = [q_ref[:, (kv * GRP + g) * HEAD_DIM:(kv * GRP + g + 1) * HEAD_DIM] for g in range(GRP)]
        qs = jnp.concatenate(heads, axis=0).astype(BF16)
        s = lax.dot_general(qs, k, nt, preferred_element_type=F32)
        m = jnp.max(s, axis=-1, keepdims=True)
        if has_ctx:
            s_ctx = lax.dot_general(qs, kc_ref[:, sl].astype(BF16), nt, preferred_element_type=F32)
            m = jnp.maximum(m, jnp.max(s_ctx, axis=-1, keepdims=True))
        p = jnp.exp(s - m)
        den = jnp.sum(p, axis=-1, keepdims=True)
        o = jnp.dot(p.astype(BF16), v, preferred_element_type=F32)
        if has_ctx:
            p_ctx = jnp.exp(s_ctx - m)
            den = den + jnp.sum(p_ctx, axis=-1, keepdims=True)
            o = o + jnp.dot(p_ctx.astype(BF16), vc_ref[:, sl].astype(BF16), preferred_element_type=F32)
        o = o / den
        for g in range(GRP):
            h = kv * GRP + g
            o_ref[:, h * HEAD_DIM:(h + 1) * HEAD_DIM] = o[g * tq:(g + 1) * tq]


def _attention(layer, q, k_rot, v, cache_k4, cache_v4):
    tq = TQ_PROMPT
    blk = lambda w: pl.BlockSpec((tq, w), lambda b: (b, 0))
    attn_p = pl.pallas_call(
        functools.partial(_attn_kernel, tq=tq, has_ctx=False),
        grid=(BATCH,),
        in_specs=[blk(ATTN_WIDTH), blk(KV_WIDTH), blk(KV_WIDTH)],
        out_specs=blk(ATTN_WIDTH),
        out_shape=jax.ShapeDtypeStruct((P_ROWS, ATTN_WIDTH), F32),
        compiler_params=_params(1, 32),
        name="attn_prompt",
    )(q, k_rot, v)

    tq = TQ_SAMPLE
    q_tiles = DEC_SEQ // tq
    own = pl.BlockSpec((DEC_SEQ, KV_WIDTH), lambda b, j: (P_ROWS // DEC_SEQ + b, 0))
    ctx = pl.BlockSpec((None, None, PAST_LEN, KV_WIDTH), lambda b, j: (b, layer, 0, 0))
    attn_s = pl.pallas_call(
        functools.partial(_attn_kernel, tq=tq, has_ctx=True),
        grid=(DEC_BATCH, q_tiles),
        in_specs=[pl.BlockSpec((tq, ATTN_WIDTH), lambda b, j: (P_ROWS // tq + b * q_tiles + j, 0)),
                  own, own, ctx, ctx],
        out_specs=pl.BlockSpec((tq, ATTN_WIDTH), lambda b, j: (b * q_tiles + j, 0)),
        out_shape=jax.ShapeDtypeStruct((S_ROWS, ATTN_WIDTH), F32),
        compiler_params=_params(2, 48),
        name="attn_sample",
    )(q, k_rot, v, cache_k4, cache_v4)
    return attn_p, attn_s


def _scan_item(i):
    j = i - BATCH
    seq = jnp.where(i < BATCH, i, BATCH + j // S_CHUNKS)
    bwd = jnp.where(i < BATCH, i, BATCH + (j // S_CHUNKS) * S_CHUNKS + (S_CHUNKS - 1) - j % S_CHUNKS)
    return seq, i, bwd


def _s5_kernel(uf_ref, ub_ref, h0_ref, lam_ref, b_ref, c_ref, yf_ref, yb_ref, fin_ref, drive, slab, carry):
    tc, pitch = SCAN_CHUNK, SCAN_PITCH
    i = pl.program_id(0)

    @pl.when((i < BATCH) | ((i - BATCH) % S_CHUNKS == 0))
    def _():
        carry[...] = h0_ref[0]

    def section(d, part):
        return (2 * d + part) * N_GP * pitch

    for d, u_ref in ((0, uf_ref), (1, ub_ref)):
        for ch in range(SSM_WIDTH // LANES):
            u = u_ref[:, ch * LANES:(ch + 1) * LANES].astype(BF16)
            for gp in range(ch * GP_PER_CHUNK, (ch + 1) * GP_PER_CHUNK):
                bu = jnp.dot(u, b_ref[d, gp], preferred_element_type=F32)
                drive[pl.ds(section(d, 0) + gp * pitch, tc), :] = bu[:, :LANES]
                drive[pl.ds(section(d, 1) + gp * pitch, tc), :] = bu[:, LANES:]

    lam = [[lam_ref[d, part] for part in range(2)] for d in range(2)]

    def step(t, h):
        out = []
        for d in range(2):
            tt = t if d == 0 else tc - 1 - t
            rows_re = pl.ds(section(d, 0) + tt, N_GP, stride=pitch)
            rows_im = pl.ds(section(d, 1) + tt, N_GP, stride=pitch)
            ar, ai = lam[d]
            hr, hi = h[2 * d], h[2 * d + 1]
            nr = ar * hr - ai * hi + drive[rows_re, :]
            ni = ar * hi + ai * hr + drive[rows_im, :]
            slab[rows_re, :] = nr
            slab[rows_im, :] = ni
            out += [nr, ni]
        return tuple(out)

    h = lax.fori_loop(0, tc, step, tuple(carry[s] for s in range(4)), unroll=SCAN_UNROLL)
    for s in range(4):
        carry[s] = h[s]
        fin_ref[0, s] = h[s]

    for d, y_ref in ((0, yf_ref), (1, yb_ref)):
        for ch in range(SSM_WIDTH // LANES):
            y = None
            for gp in range(ch * GP_PER_CHUNK, (ch + 1) * GP_PER_CHUNK):
                hs = jnp.concatenate([slab[pl.ds(section(d, 0) + gp * pitch, tc), :],
                                      slab[pl.ds(section(d, 1) + gp * pitch, tc), :]], axis=1).astype(BF16)
                part = jnp.dot(hs, c_ref[d, gp], preferred_element_type=F32)
                y = part if y is None else y + part
            y_ref[:, ch * LANES:(ch + 1) * LANES] = y


def _s5_scan(layer, u, h0, lam_bar, b_pad, c_pad):
    tc = SCAN_CHUNK
    n_items = BATCH + DEC_BATCH * S_CHUNKS
    n_seq = BATCH + DEC_BATCH
    fwd = pl.BlockSpec((tc, SSM_WIDTH), lambda i: (_scan_item(i)[1], 0))
    bwd = pl.BlockSpec((tc, SSM_WIDTH), lambda i: (_scan_item(i)[2], 0))
    state = pl.BlockSpec((1, 4, N_GP, LANES), lambda i: (_scan_item(i)[0], 0, 0, 0))
    state_in = pl.BlockSpec((1, None, 4, N_GP, LANES), lambda i: (_scan_item(i)[0], layer, 0, 0, 0))
    return pl.pallas_call(
        _s5_kernel,
        grid=(n_items,),
        in_specs=[fwd, bwd, state_in, _layer_spec(lam_bar, layer), _layer_spec(b_pad, layer),
                  _layer_spec(c_pad, layer)],
        out_specs=[fwd, bwd, state],
        out_shape=[jax.ShapeDtypeStruct((ROWS, SSM_WIDTH), F32),
                   jax.ShapeDtypeStruct((ROWS, SSM_WIDTH), F32),
                   jax.ShapeDtypeStruct((n_seq, 4, N_GP, LANES), F32)],
        scratch_shapes=[pltpu.VMEM((4 * N_GP * SCAN_PITCH, LANES), F32),
                        pltpu.VMEM((4 * N_GP * SCAN_PITCH, LANES), F32),
                        pltpu.VMEM((4, N_GP, LANES), F32)],
        compiler_params=_params(1, 48),
        name="s5_scan",
    )(u, u, h0, lam_bar, b_pad, c_pad)


def _s5_discretise(lam_re, lam_im, log_step, b_re, b_im, c_re, c_im):
    delta = jnp.exp(log_step)[..., None]
    mag = jnp.exp(lam_re * delta)
    lbr = mag * jnp.cos(lam_im * delta)
    lbi = mag * jnp.sin(lam_im * delta)
    den = lam_re * lam_re + lam_im * lam_im
    cr = ((lbr - 1.0) * lam_re + lbi * lam_im) / den
    ci = (lbi * lam_re - (lbr - 1.0) * lam_im) / den
    bbr = cr[..., None] * b_re - ci[..., None] * b_im
    bbi = cr[..., None] * b_im + ci[..., None] * b_re
    lam_bar = jnp.stack([lbr, lbi], axis=2).reshape(DEPTH, 2, 2, N_GP, LANES)

    eye2 = jnp.eye(2, dtype=F32)
    slot = jax.nn.one_hot(jnp.arange(N_GP) % GP_PER_CHUNK, GP_PER_CHUNK, dtype=F32)
    bb = jnp.stack([bbr, bbi], axis=0).reshape(2, DEPTH, 2, N_GP, 2, SSM_STATE, SSM_GROUP_CH)
    b_small = jnp.einsum('cldgjph,jk->ldgjhckp', bb, eye2).reshape(DEPTH, 2, N_GP, 2 * SSM_GROUP_CH, 2 * LANES)
    b_pad = jnp.einsum('ldgrc,gs->ldgsrc', b_small, slot).reshape(DEPTH, 2, N_GP, LANES, 2 * LANES)
    cc = jnp.stack([c_re, -c_im], axis=0).reshape(2, DEPTH, 2, N_GP, 2, SSM_GROUP_CH, SSM_STATE)
    c_small = jnp.einsum('cldgjhp,jk->ldgckpjh', cc, eye2).reshape(DEPTH, 2, N_GP, 2 * LANES, 2 * SSM_GROUP_CH)
    c_pad = jnp.einsum('ldgrc,gs->ldgrsc', c_small, slot).reshape(DEPTH, 2, N_GP, 2 * LANES, LANES)
    return lam_bar, b_pad.astype(BF16), c_pad.astype(BF16)


def _outproj_kernel(*refs, moe):
    (attn_p_ref, attn_s_ref, yf_ref, yb_ref, u_ref, xp_ref, xs_ref, g1_ref, sh2_ref, sc2_ref, d_ref, wglu_ref,
     bglu_ref, wout_ref, lng_ref, lnb_ref) = refs[:16]
    if moe:
        router_ref, x1_ref, h2_ref, meta_ref, meta_t_ref, cnt_ref, wglu_b, wout_b, cnt = refs[16:]
    else:
        x1_ref, h2_ref, wglu_b, wout_b = refs[16:]
    attn = _group_rows(attn_p_ref, attn_s_ref)
    x = _group_rows(xp_ref, xs_ref)

    @pl.when(pl.program_id(0) == 0)
    def _():
        wglu_b[...] = wglu_ref[...].astype(BF16)
        wout_b[...] = wout_ref[...].astype(BF16)

    y = yf_ref[...] + yb_ref[...] + d_ref[...] * u_ref[...]
    g = 0.5 * y * (1.0 + jnp.tanh(math.sqrt(2.0 / math.pi) * (y + 0.044715 * (y * y * y))))
    z = jnp.dot(g.astype(BF16), wglu_b[...], preferred_element_type=F32) + bglu_ref[...]
    y_ssm = g * _sigmoid(z)
    mix = (jnp.dot(attn.astype(BF16), wout_b[:ATTN_WIDTH, :], preferred_element_type=F32)
           + jnp.dot(y_ssm.astype(BF16), wout_b[ATTN_WIDTH:, :], preferred_element_type=F32))
    x1 = _layer_norm(DEEPNORM_ALPHA * x + g1_ref[0] * mix) * lng_ref[...] + lnb_ref[...]
    x1_ref[...] = x1
    h2 = _layer_norm(x1) * (1.0 + sc2_ref[0]) + sh2_ref[0]
    h2_ref[...] = h2

    if moe:
        @pl.when(pl.program_id(0) == 0)
        def _():
            cnt[...] = jnp.zeros_like(cnt)

        router = router_ref[...]
        h_hi, r_hi = h2.astype(BF16), router.astype(BF16)
        h_lo = (h2 - h_hi.astype(F32)).astype(BF16)
        r_lo = (router - r_hi.astype(F32)).astype(BF16)
        logits = (jnp.dot(h_hi, r_hi, preferred_element_type=F32)
                  + (jnp.dot(h_hi, r_lo, preferred_element_type=F32)
                     + jnp.dot(h_lo, r_hi, preferred_element_type=F32)))
        tm = logits.shape[0]
        lane = lax.broadcasted_iota(jnp.int32, logits.shape, 1).astype(F32)
        neg = jnp.float32(-jnp.inf)
        l1 = jnp.where(lane < N_EXPERTS, logits, neg)
        m1 = jnp.max(l1, axis=-1, keepdims=True)
        i1 = jnp.min(jnp.where(l1 == m1, lane, float(LANES)), axis=-1, keepdims=True)
        l2 = jnp.where(lane == i1, neg, l1)
        m2 = jnp.max(l2, axis=-1, keepdims=True)
        i2 = jnp.min(jnp.where(l2 == m2, lane, float(LANES)), axis=-1, keepdims=True)
        e2 = jnp.exp(m2 - m1)
        w1 = 1.0 / (1.0 + e2)
        w2 = e2 / (1.0 + e2)
        hit = jnp.where((lane == i1) | (lane == i2), 1.0, 0.0)
        r = lax.broadcasted_iota(jnp.int32, (tm, tm), 0)
        c = lax.broadcasted_iota(jnp.int32, (tm, tm), 1)
        earlier = jnp.where(c < r, 1.0, 0.0).astype(BF16)
        rank = cnt[...] + jnp.dot(earlier, hit.astype(BF16), preferred_element_type=F32)
        r1 = jnp.sum(jnp.where(lane == i1, rank, 0.0), axis=-1, keepdims=True)
        r2 = jnp.sum(jnp.where(lane == i2, rank, 0.0), axis=-1, keepdims=True)
        cnt[...] = cnt[...] + jnp.sum(hit, axis=0, keepdims=True)
        cnt_ref[...] = cnt[...]
        fields = (i1, i2, r1, r2, w1, w2)
        meta = jnp.zeros_like(logits)
        for k, f in enumerate(fields):
            meta = jnp.where(lane == float(k), f, meta)
        meta_ref[...] = meta
        meta_t_ref[...] = meta.T[:meta_t_ref.shape[0], :]


def _out_projection(layer, attn_p, attn_s, yf, yb, u, x_pair, mod3, d_skip, w_glu, b_glu, w_out, ln_g, ln_b,
                    router):
    tm = TM_PROJ
    moe = router is not None
    row = lambda w: pl.BlockSpec((tm, w), lambda i: (i, 0))
    full = lambda a: pl.BlockSpec(a.shape, lambda i: (0,) * a.ndim)
    args = [attn_p, attn_s, yf, yb, u, *x_pair, mod3, mod3, mod3, d_skip, w_glu, b_glu, w_out, ln_g, ln_b]
    in_specs = (_group_specs(tm, ATTN_WIDTH, False)
                + [row(SSM_WIDTH), row(SSM_WIDTH), row(SSM_WIDTH)]
                + _group_specs(tm, D_MODEL, x_pair[0] is x_pair[1])
                + [_mod_spec(layer, 2, tm), _mod_spec(layer, 3, tm), _mod_spec(layer, 4, tm),
                   full(d_skip), _layer_spec(w_glu, layer), full(b_glu), _layer_spec(w_out, layer),
                   full(ln_g), full(ln_b)])
    out_specs = [row(D_MODEL), row(D_MODEL)]
    out_shape = [jax.ShapeDtypeStruct((ROWS, D_MODEL), F32), jax.ShapeDtypeStruct((ROWS, D_MODEL), F32)]
    scratch = [pltpu.VMEM((SSM_WIDTH, SSM_WIDTH), BF16), pltpu.VMEM((D_MODEL, D_MODEL), BF16)]
    if moe:
        args.append(router)
        in_specs.append(_layer_spec(router, layer // 2))
        out_specs += [row(LANES), pl.BlockSpec((META_ROWS, tm), lambda i: (0, i)),
                      pl.BlockSpec((1, LANES), lambda i: (0, 0))]
        out_shape += [jax.ShapeDtypeStruct((ROWS, LANES), F32), jax.ShapeDtypeStruct((META_ROWS, ROWS), F32),
                      jax.ShapeDtypeStruct((1, LANES), F32)]
        scratch.append(pltpu.VMEM((1, LANES), F32))
    return pl.pallas_call(
        functools.partial(_outproj_kernel, moe=moe),
        grid=(ROWS // tm,),
        in_specs=in_specs,
        out_specs=out_specs,
        out_shape=out_shape,
        scratch_shapes=scratch,
        compiler_params=_params(1, 48),
        name="out_projection",
    )(*args)


def _swiglu_partial(xb, wg, wu, wd):
    a = jnp.dot(xb, wg, preferred_element_type=F32)
    b = jnp.dot(xb, wu, preferred_element_type=F32)
    act = a * _sigmoid(a) * b
    return act, lambda act_: jnp.dot(act_.astype(BF16), wd, preferred_element_type=F32)


def _ffn_kernel(h_ref, x1_ref, g2_ref, wg_ref, wu_ref, wd_ref, lng_ref, lnb_ref, o_ref, hb, acc, *, tf, d_ff):
    j = pl.program_id(1)

    @pl.when(j == 0)
    def _():
        hb[...] = h_ref[...].astype(BF16)
        acc[...] = jnp.zeros_like(acc)

    last = pl.num_programs(1) - 1
    tail = d_ff % tf

    def block(width, valid):
        wd = wd_ref[:width, :]
        if valid < width:
            r = lax.broadcasted_iota(jnp.int32, wd.shape, 0)
            wd = jnp.where(r < valid, wd, 0.0)
        act, down = _swiglu_partial(hb[...], wg_ref[:, :width].astype(BF16), wu_ref[:, :width].astype(BF16),
                                    wd.astype(BF16))
        if valid < width:
            col = lax.broadcasted_iota(jnp.int32, act.shape, 1)
            act = jnp.where(col < valid, act, 0.0)
        acc[...] += down(act)

    if tail:
        @pl.when(j < last)
        def _():
            block(tf, tf)

        @pl.when(j == last)
        def _():
            block(-(-tail // LANES) * LANES, tail)
    else:
        block(tf, tf)

    @pl.when(j == last)
    def _():
        y = DEEPNORM_ALPHA * x1_ref[...] + g2_ref[0] * acc[...]
        o_ref[...] = _layer_norm(y) * lng_ref[...] + lnb_ref[...]


def _ffn(layer, h2, x1, mod3, wg, wu, wd, ln_g, ln_b):
    tm, tf, d_ff = TM_FFN, TF_DENSE, D_FF
    w_idx = layer // 2
    up = pl.BlockSpec((None, D_MODEL, tf), lambda i, j: (w_idx, 0, j))
    down = pl.BlockSpec((None, tf, D_MODEL), lambda i, j: (w_idx, j, 0))
    row = lambda w: pl.BlockSpec((tm, w), lambda i, j: (i, 0))
    vec = pl.BlockSpec((1, D_MODEL), lambda i, j: (0, 0))
    g2 = pl.BlockSpec((1, 1, D_MODEL), lambda i, j: (layer * N_MOD + _mod_row(i, tm), 0, 5))
    return pl.pallas_call(
        functools.partial(_ffn_kernel, tf=tf, d_ff=d_ff),
        grid=(ROWS // tm, pl.cdiv(d_ff, tf)),
        in_specs=[row(D_MODEL), row(D_MODEL), g2, up, up, down, vec, vec],
        out_specs=row(D_MODEL),
        out_shape=jax.ShapeDtypeStruct((ROWS, D_MODEL), F32),
        scratch_shapes=[pltpu.VMEM((tm, D_MODEL), BF16), pltpu.VMEM((tm, D_MODEL), F32)],
        compiler_params=_params(2, 48),
        name="ffn_dense",
    )(h2, x1, mod3, wg, wu, wd, ln_g, ln_b)


def _route_plan(meta_t, counts):
    i32 = jnp.int32
    e1, e2 = meta_t[0].astype(i32), meta_t[1].astype(i32)
    r1, r2 = meta_t[2].astype(i32), meta_t[3].astype(i32)
    cnt = counts[0, :N_EXPERTS].astype(i32)
    tiles = (cnt + MOE_TILE - 1) // MOE_TILE
    tile_end = jnp.cumsum(tiles)
    tile_start = tile_end - tiles
    base = tile_start * MOE_TILE
    pos1 = (base[e1] + r1) * ROW_TILE
    pos2 = (base[e2] + r2) * ROW_TILE
    t = jnp.arange(MOE_TILES, dtype=i32)
    n_used = tile_end[-1]
    used = t < n_used
    owner = jnp.sum((t[:, None] >= tile_end[None, :]).astype(i32), axis=1)
    owner = jnp.minimum(owner, N_EXPERTS - 1)
    rows = jnp.clip(cnt[owner] - (t - tile_start[owner]) * MOE_TILE, 0, MOE_TILE)
    rows = jnp.where(used, rows, 0)
    n_sub = (rows + MOE_SUB - 1) // MOE_SUB
    last_owner = owner[jnp.maximum(n_used - 1, 0)]
    tile_expert = jnp.where(used, owner, last_owner)
    zero_fill = (rows < MOE_TILE).astype(i32)
    return pos1, pos2, tile_expert, n_sub, zero_fill


def _slot_rows(start):
    return pl.ds(pl.multiple_of(start, ROW_TILE), ROW_TILE)


def _dispatch_kernel(zf_ref, p1_ref, p2_ref, h_ref, xs_ref, src, zbuf, zsem, sem):
    tm = h_ref.shape[0]
    tile_rows = MOE_TILE * ROW_TILE

    def zero_copy(k):
        return pltpu.make_async_copy(zbuf, xs_ref.at[pl.ds(k * tile_rows, tile_rows)], zsem)

    @pl.when(pl.program_id(0) == 0)
    def _():
        zbuf[...] = jnp.zeros_like(zbuf)
        for k in range(MOE_TILES):
            @pl.when(zf_ref[k] != 0)
            def _():
                zero_copy(k).start()
        for k in range(MOE_TILES):
            @pl.when(zf_ref[k] != 0)
            def _():
                zero_copy(k).wait()

    for s in range(ROW_TILE):
        src[pl.ds(s, tm, stride=ROW_TILE), :] = h_ref[:, s * LANES:(s + 1) * LANES]

    def row_copy(r, p_ref):
        return pltpu.make_async_copy(src.at[_slot_rows(r * ROW_TILE)], xs_ref.at[_slot_rows(p_ref[r])], sem)

    def start(r, carry):
        row_copy(r, p1_ref).start()
        row_copy(r, p2_ref).start()
        return carry

    def wait(r, carry):
        row_copy(r, p1_ref).wait()
        row_copy(r, p2_ref).wait()
        return carry

    lax.fori_loop(0, tm, start, 0, unroll=DMA_UNROLL)
    lax.fori_loop(0, tm, wait, 0, unroll=DMA_UNROLL)


def _moe_dispatch(h2, pos1, pos2, zero_fill):
    tm = TM_ROUTE
    pos = pl.BlockSpec((tm,), lambda i, zf: (i,), memory_space=pltpu.SMEM)
    return pl.pallas_call(
        _dispatch_kernel,
        grid_spec=pltpu.PrefetchScalarGridSpec(
            num_scalar_prefetch=1,
            grid=(ROWS // tm,),
            in_specs=[pos, pos, pl.BlockSpec((tm, D_MODEL), lambda i, zf: (i, 0))],
            out_specs=pl.BlockSpec(memory_space=pl.ANY),
            scratch_shapes=[pltpu.VMEM((tm * ROW_TILE, LANES), F32),
                            pltpu.VMEM((MOE_TILE * ROW_TILE, LANES), F32),
                            pltpu.SemaphoreType.DMA(()), pltpu.SemaphoreType.DMA(())]),
        out_shape=jax.ShapeDtypeStruct((MOE_ROWS * ROW_TILE, LANES), F32),
        compiler_params=_params(1, 32),
        name="moe_dispatch",
    )(zero_fill, pos1, pos2, h2)


def _expert_kernel(te_ref, ns_ref, xs_ref, wg_ref, wu_ref, wd_ref, ys_ref, xb, acc):
    i, j = pl.program_id(0), pl.program_id(1)
    n_sub = ns_ref[i]

    @pl.when(j == 0)
    def _():
        for s in range(ROW_TILE):
            xb[:, s * LANES:(s + 1) * LANES] = xs_ref[pl.ds(s, MOE_TILE, stride=ROW_TILE), :].astype(BF16)
        acc[...] = jnp.zeros_like(acc)

    def accumulate(rows):
        act, down = _swiglu_partial(xb[rows, :], wg_ref[...].astype(BF16), wu_ref[...].astype(BF16),
                                    wd_ref[...].astype(BF16))
        acc[rows, :] += down(act)

    full = MOE_TILE // MOE_SUB

    @pl.when(n_sub == full)
    def _():
        accumulate(slice(None))

    @pl.when((n_sub > 0) & (n_sub < full))
    def _():
        def body(s, carry):
            accumulate(pl.ds(pl.multiple_of(s * MOE_SUB, MOE_SUB), MOE_SUB))
            return carry
        lax.fori_loop(0, n_sub, body, 0)

    @pl.when(j == pl.num_programs(1) - 1)
    def _():
        for s in range(ROW_TILE):
            ys_ref[pl.ds(s, MOE_TILE, stride=ROW_TILE), :] = acc[:, s * LANES:(s + 1) * LANES]


def _moe_experts(layer, xs, tile_expert, n_sub, wg, wu, wd):
    tf = TF_MOE
    n_j = EXPERT_FF // tf
    w_idx = layer // 2
    ff = lambda i, j, ns: jnp.where(ns[i] > 0, j, n_j - 1)
    up = pl.BlockSpec((None, None, D_MODEL, tf), lambda i, j, te, ns: (w_idx, te[i], 0, ff(i, j, ns)))
    down = pl.BlockSpec((None, None, tf, D_MODEL), lambda i, j, te, ns: (w_idx, te[i], ff(i, j, ns), 0))
    rows = pl.BlockSpec((MOE_TILE * ROW_TILE, LANES), lambda i, j, te, ns: (i, 0))
    return pl.pallas_call(
        _expert_kernel,
        grid_spec=pltpu.PrefetchScalarGridSpec(
            num_scalar_prefetch=2,
            grid=(MOE_TILES, n_j),
            in_specs=[rows, up, up, down],
            out_specs=rows,
            scratch_shapes=[pltpu.VMEM((MOE_TILE, D_MODEL), BF16), pltpu.VMEM((MOE_TILE, D_MODEL), F32)]),
        out_shape=jax.ShapeDtypeStruct((MOE_ROWS * ROW_TILE, LANES), F32),
        compiler_params=_params(2, 48),
        name="moe_experts",
    )(tile_expert, n_sub, xs, wg, wu, wd)


def _combine_kernel(p1_ref, p2_ref, ys_ref, x1_ref, meta_ref, g2_ref, lng_ref, lnb_ref, *rest, split):
    b1, b2, f, sem = rest[-4:]
    tm = x1_ref.shape[0]

    def row_copy(r, p_ref, buf):
        return pltpu.make_async_copy(ys_ref.at[_slot_rows(p_ref[r])], buf.at[_slot_rows(r * ROW_TILE)], sem)

    def start(r, carry):
        row_copy(r, p1_ref, b1).start()
        row_copy(r, p2_ref, b2).start()
        return carry

    def wait(r, carry):
        row_copy(r, p1_ref, b1).wait()
        row_copy(r, p2_ref, b2).wait()
        return carry

    lax.fori_loop(0, tm, start, 0, unroll=DMA_UNROLL)
    lax.fori_loop(0, tm, wait, 0, unroll=DMA_UNROLL)
    meta = meta_ref[...]
    w1, w2 = meta[:, 4:5], meta[:, 5:6]
    for s in range(ROW_TILE):
        rows = pl.ds(s, tm, stride=ROW_TILE)
        f[:, s * LANES:(s + 1) * LANES] = w1 * b1[rows, :] + w2 * b2[rows, :]
    y = DEEPNORM_ALPHA * x1_ref[...] + g2_ref[0] * f[...]
    out = _layer_norm(y) * lng_ref[...] + lnb_ref[...]
    if split:
        op_ref, os_ref = rest[:2]
        is_prompt = pl.program_id(0) < P_ROWS // tm

        @pl.when(is_prompt)
        def _():
            op_ref[...] = out

        @pl.when(jnp.logical_not(is_prompt))
        def _():
            os_ref[...] = out
    else:
        rest[0][...] = out


def _moe_combine(layer, ys, pos1, pos2, x1, meta, mod3, ln_g, ln_b, split):
    tm = TM_COMBINE
    pos = pl.BlockSpec((tm,), lambda i: (i,), memory_space=pltpu.SMEM)
    row = lambda w: pl.BlockSpec((tm, w), lambda i: (i, 0))
    vec = pl.BlockSpec((1, D_MODEL), lambda i: (0, 0))
    if split:
        out_specs = _group_specs(tm, D_MODEL, False)
        out_shape = [jax.ShapeDtypeStruct((P_ROWS, D_MODEL), F32), jax.ShapeDtypeStruct((S_ROWS, D_MODEL), F32)]
    else:
        out_specs = row(D_MODEL)
        out_shape = jax.ShapeDtypeStruct((ROWS, D_MODEL), F32)
    return pl.pallas_call(
        functools.partial(_combine_kernel, split=split),
        grid=(ROWS // tm,),
        in_specs=[pos, pos, pl.BlockSpec(memory_space=pl.ANY), row(D_MODEL), row(LANES),
                  _mod_spec(layer, 5, tm), vec, vec],
        out_specs=out_specs,
        out_shape=out_shape,
        scratch_shapes=[pltpu.VMEM((tm * ROW_TILE, LANES), F32), pltpu.VMEM((tm * ROW_TILE, LANES), F32),
                        pltpu.VMEM((tm, D_MODEL), F32), pltpu.SemaphoreType.DMA(())],
        compiler_params=_params(1, 32),
        name="moe_combine",
    )(pos1, pos2, ys, x1, meta, mod3, ln_g, ln_b)


def _rope_tables():
    rows = DEC_SEQ // GRID_W
    row = jnp.repeat(jnp.arange(rows), GRID_W).astype(F32)
    col = jnp.tile(jnp.arange(GRID_W), rows).astype(F32)
    freqs = ROPE_THETA ** (-jnp.arange(ROPE_PAIRS, dtype=F32) / ROPE_PAIRS)
    ang = jnp.stack([row[:, None] * freqs, col[:, None] * freqs], axis=1)
    cos, sin = jnp.cos(ang), jnp.sin(ang)
    cos_h = jnp.concatenate([cos[:, 0], cos[:, 0], cos[:, 1], cos[:, 1]], axis=-1)
    sin_h = jnp.concatenate([-sin[:, 0], sin[:, 0], -sin[:, 1], sin[:, 1]], axis=-1)
    per_tile = LANES // HEAD_DIM
    cos_t = jnp.concatenate([jnp.ones((ROPE_ID_ROWS, LANES), F32), jnp.tile(cos_h, (1, per_tile))], axis=0)
    sin_t = jnp.concatenate([jnp.zeros((ROPE_ID_ROWS, LANES), F32), jnp.tile(sin_h, (1, per_tile))], axis=0)
    return cos_t, sin_t


def kernel(x_prompt, x_sample, c, cache_k, cache_v, state_ssm_re, state_ssm_im, c_ctx, w_ada, b_ada,
           w_in, q_gain, k_gain, ssm_lambda_re, ssm_lambda_im, ssm_log_step, ssm_b_re, ssm_b_im,
           ssm_c_re, ssm_c_im, ssm_d, w_glu, b_glu, w_out, ln1_g, ln1_b, ln2_g, ln2_b,
           ffn_w_gate, ffn_w_up, ffn_w_down, router_w, moe_w_gate, moe_w_up, moe_w_down):
    x_pair = (x_prompt.reshape(P_ROWS, D_MODEL), x_sample.reshape(S_ROWS, D_MODEL))
    cvec = jnp.concatenate([c_ctx[None], c, jnp.zeros((N_MOD - 1 - DEC_BATCH, D_MODEL), F32)], axis=0)
    mod3 = _modulation(cvec, w_ada, b_ada).reshape(DEPTH * N_MOD, 1, 6 * D_MODEL)
    rope_cos, rope_sin = _rope_tables()
    lam_bar, b_pad, c_pad = _s5_discretise(ssm_lambda_re, ssm_lambda_im, ssm_log_step,
                                           ssm_b_re, ssm_b_im, ssm_c_re, ssm_c_im)
    cache_k4 = cache_k.reshape(DEC_BATCH, DEPTH, PAST_LEN, KV_WIDTH)
    cache_v4 = cache_v.reshape(DEC_BATCH, DEPTH, PAST_LEN, KV_WIDTH)
    h0_s = jnp.stack([state_ssm_re, state_ssm_im], axis=3)
    h0_s = h0_s.reshape(DEC_BATCH, DEPTH, 4, N_GP, LANES)
    h0_all = jnp.concatenate([jnp.zeros((BATCH, DEPTH, 4, N_GP, LANES), F32), h0_s], axis=0)
    router_pad = jnp.pad(router_w, ((0, 0), (0, 0), (0, LANES - N_EXPERTS)))
    gain2 = lambda g: jnp.tile(g, (1, LANES // HEAD_DIM))

    new_k, new_v, new_re, new_im = [], [], [], []
    for l in range(DEPTH):
        q, k_rot, k_norm, v, u = _in_projection(l, x_pair, mod3, w_in, gain2(q_gain[l:l + 1]),
                                                gain2(k_gain[l:l + 1]), rope_cos, rope_sin)
        attn_p, attn_s = _attention(l, q, k_rot, v, cache_k4, cache_v4)
        yf, yb, fin = _s5_scan(l, u, h0_all, lam_bar, b_pad, c_pad)
        moe = l % 2 == 1
        outs = _out_projection(l, attn_p, attn_s, yf, yb, u, x_pair, mod3, ssm_d[l:l + 1], w_glu,
                               b_glu[l:l + 1], w_out, ln1_g[l:l + 1], ln1_b[l:l + 1],
                               router_pad if moe else None)
        if moe:
            x1, h2, meta, meta_t, counts = outs
            pos1, pos2, tile_expert, n_sub, zero_fill = _route_plan(meta_t, counts)
            xs = _moe_dispatch(h2, pos1, pos2, zero_fill)
            ys = _moe_experts(l, xs, tile_expert, n_sub, moe_w_gate, moe_w_up, moe_w_down)
            last = l == DEPTH - 1
            x = _moe_combine(l, ys, pos1, pos2, x1, meta, mod3, ln2_g[l:l + 1], ln2_b[l:l + 1], split=last)
            x_pair = tuple(x) if last else (x, x)
        else:
            x1, h2 = outs
            x = _ffn(l, h2, x1, mod3, ffn_w_gate, ffn_w_up, ffn_w_down, ln2_g[l:l + 1], ln2_b[l:l + 1])
            x_pair = (x, x)
        new_k.append(k_norm[:P_ROWS].reshape(BATCH, SEQ, N_KV_HEADS, HEAD_DIM))
        new_v.append(v[:P_ROWS].reshape(BATCH, SEQ, N_KV_HEADS, HEAD_DIM))
        fin_p = fin[:BATCH].reshape(BATCH, 2, 2, SSM_GROUPS, SSM_STATE)
        new_re.append(fin_p[:, :, 0])
        new_im.append(fin_p[:, :, 1])

    assert DEPTH % 2 == 0, "the last layer is the routed one and returns the two groups separately"
    y_prompt = x_pair[0].reshape(BATCH, SEQ, D_MODEL)
    y_sample = x_pair[1].reshape(DEC_BATCH, DEC_SEQ, D_MODEL)
    return (y_prompt, y_sample, jnp.stack(new_k, axis=1), jnp.stack(new_v, axis=1),
            jnp.stack(new_re, axis=1), jnp.stack(new_im, axis=1))
```

```python
import functools
import math

import jax
import jax.numpy as jnp
from jax import lax
from jax.experimental import pallas as pl
from jax.experimental.pallas import tpu as pltpu

D_MODEL = 1024
BATCH = 16
SEQ = 256
DEPTH = 4
DEC_BATCH = 2
DEC_SEQ = 2048
PAST_LEN = 256
GRID_W = 64
N_HEADS = 8
N_KV_HEADS = 2
HEAD_DIM = 64
ATTN_WIDTH = N_HEADS * HEAD_DIM
KV_WIDTH = N_KV_HEADS * HEAD_DIM
SSM_WIDTH = D_MODEL - ATTN_WIDTH
SSM_GROUP_CH = 16
SSM_GROUPS = SSM_WIDTH // SSM_GROUP_CH
SSM_STATE = 64
IN_WIDTH = ATTN_WIDTH + 2 * KV_WIDTH + SSM_WIDTH
ROPE_THETA = 10000.0
ROPE_PAIRS = HEAD_DIM // 4
D_FF = 2752
N_EXPERTS = 8
TOP_K = 2
EXPERT_FF = 3584
DEEPNORM_ALPHA = (2 * DEPTH) ** 0.25
NORM_EPS = 1e-6

F32 = jnp.float32
BF16 = jnp.bfloat16
HIGHEST = lax.Precision.HIGHEST
LANES = 128

P_ROWS = BATCH * SEQ
S_ROWS = DEC_BATCH * DEC_SEQ
ROWS = P_ROWS + S_ROWS
N_MOD = 8
GRP = N_HEADS // N_KV_HEADS
SCALE = HEAD_DIM ** -0.5

TM_PROJ = 512
ROPE_ID_ROWS = TM_PROJ
TQ_PROMPT = SEQ
TQ_SAMPLE = 128
SCAN_CHUNK = 256
SCAN_PITCH = SCAN_CHUNK + 4
N_GP = SSM_GROUPS // 2
GP_PER_CHUNK = LANES // (2 * SSM_GROUP_CH)
S_CHUNKS = DEC_SEQ // SCAN_CHUNK
N_SCAN_ITEMS = BATCH * (SEQ // SCAN_CHUNK) + DEC_BATCH * S_CHUNKS
TM_FFN = 1024
TF_DENSE = 512
TF_MOE = 512
MOE_TILE = 1024
MOE_SUB = 256
MOE_TILES = TOP_K * ROWS // MOE_TILE + N_EXPERTS
MOE_ROWS = MOE_TILES * MOE_TILE
TM_ROUTE = 1024
TM_COMBINE = 512
META_ROWS = 8
ROW_TILE = 8
DMA_UNROLL = 8
MIB = 1024 * 1024


def _params(n_axes, vmem_mib):
    return pltpu.CompilerParams(dimension_semantics=("arbitrary",) * n_axes,
                                vmem_limit_bytes=vmem_mib * MIB)


def _layer_norm(x):
    mu = jnp.mean(x, axis=-1, keepdims=True)
    xc = x - mu
    var = jnp.mean(xc * xc, axis=-1, keepdims=True)
    return xc * lax.rsqrt(var + NORM_EPS)


def _sigmoid(x):
    return 1.0 / (1.0 + jnp.exp(-x))


def _mod_row(tile, tm):
    p_tiles = P_ROWS // tm
    return jnp.where(tile < p_tiles, 0, 1 + (tile - p_tiles) // (DEC_SEQ // tm))


def _mod_kernel(c_ref, w_ref, b_ref, o_ref):
    c = c_ref[...]
    a = c * _sigmoid(c)
    o_ref[0] = jnp.dot(a, w_ref[0], precision=HIGHEST, preferred_element_type=F32) + b_ref[0]


def _modulation(cvec, w_ada, b_ada):
    tn = 1536
    return pl.pallas_call(
        _mod_kernel,
        grid=(DEPTH, 6 * D_MODEL // tn),
        in_specs=[pl.BlockSpec((N_MOD, D_MODEL), lambda l, j: (0, 0)),
                  pl.BlockSpec((1, D_MODEL, tn), lambda l, j: (l, 0, j)),
                  pl.BlockSpec((1, 1, tn), lambda l, j: (l, 0, j))],
        out_specs=pl.BlockSpec((1, N_MOD, tn), lambda l, j: (l, 0, j)),
        out_shape=jax.ShapeDtypeStruct((DEPTH, N_MOD, 6 * D_MODEL), F32),
        compiler_params=_params(2, 32),
        name="adaln_mod",
    )(cvec, w_ada, b_ada.reshape(DEPTH, 1, 6 * D_MODEL))


def _layer_spec(stacked, index):
    zeros = (0,) * (stacked.ndim - 1)
    return pl.BlockSpec((None,) + stacked.shape[1:], lambda i: (index,) + zeros)


def _mod_spec(layer, part, tm):
    return pl.BlockSpec((1, 1, D_MODEL), lambda i: (layer * N_MOD + _mod_row(i, tm), 0, part))


def _group_rows(p_ref, s_ref):
    is_prompt = pl.program_id(0) < P_ROWS // p_ref.shape[0]
    return jnp.where(is_prompt, p_ref[...], s_ref[...])


def _group_specs(tm, width, stacked):
    p_tiles = P_ROWS // tm
    offset = p_tiles if stacked else 0
    return [pl.BlockSpec((tm, width), lambda i: (jnp.minimum(i, p_tiles - 1), 0)),
            pl.BlockSpec((tm, width), lambda i: (jnp.maximum(i - p_tiles, 0) + offset, 0))]


def _inproj_kernel(xp_ref, xs_ref, sh_ref, sc_ref, w_ref, qg_ref, kg_ref, cos_ref, sin_ref,
                   q_ref, kr_ref, kn_ref, v_ref, u_ref, wb_ref):
    @pl.when(pl.program_id(0) == 0)
    def _():
        wb_ref[...] = w_ref[...].astype(BF16)

    h = _layer_norm(_group_rows(xp_ref, xs_ref)) * (1.0 + sc_ref[0]) + sh_ref[0]
    proj = jnp.dot(h.astype(BF16), wb_ref[...], preferred_element_type=F32)

    tm = proj.shape[0]
    head_of = lambda idx: lax.shift_right_logical(idx, int(math.log2(HEAD_DIM)))
    r = head_of(lax.broadcasted_iota(jnp.int32, (LANES, LANES), 0))
    c = head_of(lax.broadcasted_iota(jnp.int32, (LANES, LANES), 1))
    seg = jnp.where(r == c, 1.0 / HEAD_DIM, 0.0).astype(BF16)
    lane = lax.broadcasted_iota(jnp.int32, (tm, LANES), 1)
    first_half = (lane & (2 * ROPE_PAIRS - 1)) < ROPE_PAIRS
    cos = cos_ref[...]
    sin = sin_ref[...]

    def head_norm(t, gain):
        sq = t * t
        hi = sq.astype(BF16)
        lo = (sq - hi.astype(F32)).astype(BF16)
        ms = (jnp.dot(hi, seg, preferred_element_type=F32) + jnp.dot(lo, seg, preferred_element_type=F32))
        return t * lax.rsqrt(ms + NORM_EPS) * gain

    def rope(t):
        partner = jnp.where(first_half, pltpu.roll(t, LANES - ROPE_PAIRS, 1), pltpu.roll(t, ROPE_PAIRS, 1))
        return t * cos + partner * sin

    for j in range(ATTN_WIDTH // LANES):
        sl = slice(j * LANES, (j + 1) * LANES)
        q_ref[:, sl] = rope(head_norm(proj[:, sl], qg_ref[...])) * SCALE
    k = head_norm(proj[:, ATTN_WIDTH:ATTN_WIDTH + KV_WIDTH], kg_ref[...])
    kn_ref[...] = k
    kr_ref[...] = rope(k)
    v_ref[...] = proj[:, ATTN_WIDTH + KV_WIDTH:ATTN_WIDTH + 2 * KV_WIDTH]
    u_ref[...] = proj[:, ATTN_WIDTH + 2 * KV_WIDTH:]


def _in_projection(layer, x_pair, mod3, w_in, q_gain, k_gain, rope_cos, rope_sin):
    tm = TM_PROJ
    stacked = x_pair[0] is x_pair[1]
    p_tiles = P_ROWS // tm
    t_tiles = DEC_SEQ // tm

    def rope_idx(i):
        return (jnp.where(i < p_tiles, 0, 1 + (i - p_tiles) % t_tiles), 0)

    row = lambda w: pl.BlockSpec((tm, w), lambda i: (i, 0))
    full = lambda a: pl.BlockSpec(a.shape, lambda i: (0,) * a.ndim)
    return pl.pallas_call(
        _inproj_kernel,
        grid=(ROWS // tm,),
        in_specs=_group_specs(tm, D_MODEL, stacked) + [
                  _mod_spec(layer, 0, tm), _mod_spec(layer, 1, tm), _layer_spec(w_in, layer),
                  full(q_gain), full(k_gain),
                  pl.BlockSpec((tm, LANES), rope_idx), pl.BlockSpec((tm, LANES), rope_idx)],
        out_specs=[row(ATTN_WIDTH), row(KV_WIDTH), row(KV_WIDTH), row(KV_WIDTH), row(SSM_WIDTH)],
        out_shape=[jax.ShapeDtypeStruct((ROWS, ATTN_WIDTH), F32),
                   jax.ShapeDtypeStruct((ROWS, KV_WIDTH), F32),
                   jax.ShapeDtypeStruct((ROWS, KV_WIDTH), F32),
                   jax.ShapeDtypeStruct((ROWS, KV_WIDTH), F32),
                   jax.ShapeDtypeStruct((ROWS, SSM_WIDTH), F32)],
        scratch_shapes=[pltpu.VMEM((D_MODEL, IN_WIDTH), BF16)],
        compiler_params=_params(1, 48),
        name="in_projection",
    )(*x_pair, mod3, mod3, w_in, q_gain, k_gain, rope_cos, rope_sin)


def _attn_kernel(*refs, tq, has_ctx):
    if has_ctx:
        q_ref, k_ref, v_ref, kc_ref, vc_ref, o_ref = refs
    else:
        q_ref, k_ref, v_ref, o_ref = refs
    nt = (((1,), (1,)), ((), ()))
    for kv in range(N_KV_HEADS):
        sl = slice(kv * HEAD_DIM, (kv + 1) * HEAD_DIM)
        k = k_ref[:, sl].astype(BF16)
        v = v_ref[:, sl].astype(BF16)
        heads = [q_ref[:, (kv * GRP + g) * HEAD_DIM:(kv * GRP + g + 1) * HEAD_DIM] for g in range(GRP)]
        qs = jnp.concatenate(heads, axis=0).astype(BF16)
        s = lax.dot_general(qs, k, nt, preferred_element_type=F32)
        m = jnp.max(s, axis=-1, keepdims=True)
        if has_ctx:
            s_ctx = lax.dot_general(qs, kc_ref[:, sl].astype(BF16), nt, preferred_element_type=F32)
            m = jnp.maximum(m, jnp.max(s_ctx, axis=-1, keepdims=True))
        p = jnp.exp(s - m)
        den = jnp.sum(p, axis=-1, keepdims=True)
        o = jnp.dot(p.astype(BF16), v, preferred_element_type=F32)
        if has_ctx:
            p_ctx = jnp.exp(s_ctx - m)
            den = den + jnp.sum(p_ctx, axis=-1, keepdims=True)
            o = o + jnp.dot(p_ctx.astype(BF16), vc_ref[:, sl].astype(BF16), preferred_element_type=F32)
        o = o / den
        for g in range(GRP):
            h = kv * GRP + g
            o_ref[:, h * HEAD_DIM:(h + 1) * HEAD_DIM] = o[g * tq:(g + 1) * tq]


def _attention(layer, q, k_rot, v, cache_k4, cache_v4):
    tq = TQ_PROMPT
    blk = lambda w: pl.BlockSpec((tq, w), lambda b: (b, 0))
    attn_p = pl.pallas_call(
        functools.partial(_attn_kernel, tq=tq, has_ctx=False),
        grid=(BATCH,),
        in_specs=[blk(ATTN_WIDTH), blk(KV_WIDTH), blk(KV_WIDTH)],
        out_specs=blk(ATTN_WIDTH),
        out_shape=jax.ShapeDtypeStruct((P_ROWS, ATTN_WIDTH), F32),
        compiler_params=_params(1, 32),
        name="attn_prompt",
    )(q, k_rot, v)

    tq = TQ_SAMPLE
    q_tiles = DEC_SEQ // tq
    own = pl.BlockSpec((DEC_SEQ, KV_WIDTH), lambda b, j: (P_ROWS // DEC_SEQ + b, 0))
    ctx = pl.BlockSpec((None, None, PAST_LEN, KV_WIDTH), lambda b, j: (b, layer, 0, 0))
    attn_s = pl.pallas_call(
        functools.partial(_attn_kernel, tq=tq, has_ctx=True),
        grid=(DEC_BATCH, q_tiles),
        in_specs=[pl.BlockSpec((tq, ATTN_WIDTH), lambda b, j: (P_ROWS // tq + b * q_tiles + j, 0)),
                  own, own, ctx, ctx],
        out_specs=pl.BlockSpec((tq, ATTN_WIDTH), lambda b, j: (b * q_tiles + j, 0)),
        out_shape=jax.ShapeDtypeStruct((S_ROWS, ATTN_WIDTH), F32),
        compiler_params=_params(2, 48),
        name="attn_sample",
    )(q, k_rot, v, cache_k4, cache_v4)
    return attn_p, attn_s


def _scan_item(i):
    j = i - BATCH
    seq = jnp.where(i < BATCH, i, BATCH + j // S_CHUNKS)
    bwd = jnp.where(i < BATCH, i, BATCH + (j // S_CHUNKS) * S_CHUNKS + (S_CHUNKS - 1) - j % S_CHUNKS)
    return seq, i, bwd


def _s5_stages(uf_ref, ub_ref, lam_ref, b_ref, c_ref, yf_ref, yb_ref, h, drive_in, drive_out, slab_in, slab_out):
    tc, pitch = SCAN_CHUNK, SCAN_PITCH
    per_dot = tc // (2 * N_GP)

    def section(d, part):
        return (2 * d + part) * N_GP * pitch

    lam = [[lam_ref[d, part] for part in range(2)] for d in range(2)]
    u_bf16 = {}
    y_acc = {}
    h = list(h)
    for k in range(2 * N_GP):
        d, gp = divmod(k, N_GP)
        ch = gp // GP_PER_CHUNK
        u_ref, y_ref = (uf_ref, yf_ref) if d == 0 else (ub_ref, yb_ref)
        if (d, ch) not in u_bf16:
            u_bf16[d, ch] = u_ref[:, ch * LANES:(ch + 1) * LANES].astype(BF16)
        bu = jnp.dot(u_bf16[d, ch], b_ref[d, gp], preferred_element_type=F32)
        drive_out[pl.ds(section(d, 0) + gp * pitch, tc), :] = bu[:, :LANES]
        drive_out[pl.ds(section(d, 1) + gp * pitch, tc), :] = bu[:, LANES:]
        for t in range(k * per_dot, (k + 1) * per_dot):
            for dd in range(2):
                tt = t if dd == 0 else tc - 1 - t
                rows_re = pl.ds(section(dd, 0) + tt, N_GP, stride=pitch)
                rows_im = pl.ds(section(dd, 1) + tt, N_GP, stride=pitch)
                ar, ai = lam[dd]
                hr, hi = h[2 * dd], h[2 * dd + 1]
                nr = ar * hr - ai * hi + drive_in[rows_re, :]
                ni = ar * hi + ai * hr + drive_in[rows_im, :]
                slab_out[rows_re, :] = nr
                slab_out[rows_im, :] = ni
                h[2 * dd], h[2 * dd + 1] = nr, ni
        hs = jnp.concatenate([slab_in[pl.ds(section(d, 0) + gp * pitch, tc), :],
                              slab_in[pl.ds(section(d, 1) + gp * pitch, tc), :]], axis=1).astype(BF16)
        part = jnp.dot(hs, c_ref[d, gp], preferred_element_type=F32)
        y_acc[d, ch] = part if (d, ch) not in y_acc else y_acc[d, ch] + part
        if gp % GP_PER_CHUNK == GP_PER_CHUNK - 1:
            y_ref[:, ch * LANES:(ch + 1) * LANES] = y_acc[d, ch]
    return h


def _s5_kernel(uf_ref, ub_ref, h0_ref, lam_ref, b_ref, c_ref, yf_ref, yb_ref, fin_ref,
               drive0, drive1, slab0, slab1, carry):
    s = pl.program_id(0)

    @pl.when(s == 0)
    def _():
        for buf in (drive0, drive1, slab0, slab1, carry):
            buf[...] = jnp.zeros_like(buf)

    item = s - 1
    valid = (item >= 0) & (item < N_SCAN_ITEMS)
    first = valid & ((item < BATCH) | ((item - BATCH) % S_CHUNKS == 0))
    h = [jnp.where(first, h0_ref[0, k], carry[k]) for k in range(4)]

    def run(drive_in, drive_out, slab_in, slab_out):
        new = _s5_stages(uf_ref, ub_ref, lam_ref, b_ref, c_ref, yf_ref, yb_ref, h,
                         drive_in, drive_out, slab_in, slab_out)
        for k in range(4):
            kept = jnp.where(valid, new[k], h[k])
            carry[k] = kept
            fin_ref[0, k] = kept

    @pl.when(s % 2 == 0)
    def _():
        run(drive1, drive0, slab0, slab1)

    @pl.when(s % 2 == 1)
    def _():
        run(drive0, drive1, slab1, slab0)


def _s5_scan(layer, u, h0, lam_bar, b_pad, c_pad):
    tc = SCAN_CHUNK
    n_seq = BATCH + DEC_BATCH
    last = N_SCAN_ITEMS - 1
    drive_item = lambda s: jnp.minimum(s, last)
    scan_item = lambda s: jnp.clip(s - 1, 0, last)
    read_item = lambda s: jnp.clip(s - 2, 0, last)
    rows = lambda item, which: pl.BlockSpec((tc, SSM_WIDTH), lambda s: (_scan_item(item(s))[which], 0))
    state = pl.BlockSpec((1, 4, N_GP, LANES), lambda s: (_scan_item(scan_item(s))[0], 0, 0, 0))
    state_in = pl.BlockSpec((1, None, 4, N_GP, LANES), lambda s: (_scan_item(scan_item(s))[0], layer, 0, 0, 0))
    slab = pltpu.VMEM((4 * N_GP * SCAN_PITCH, LANES), F32)
    return pl.pallas_call(
        _s5_kernel,
        grid=(N_SCAN_ITEMS + 2,),
        in_specs=[rows(drive_item, 1), rows(drive_item, 2), state_in, _layer_spec(lam_bar, layer),
                  _layer_spec(b_pad, layer), _layer_spec(c_pad, layer)],
        out_specs=[rows(read_item, 1), rows(read_item, 2), state],
        out_shape=[jax.ShapeDtypeStruct((ROWS, SSM_WIDTH), F32),
                   jax.ShapeDtypeStruct((ROWS, SSM_WIDTH), F32),
                   jax.ShapeDtypeStruct((n_seq, 4, N_GP, LANES), F32)],
        scratch_shapes=[slab, slab, slab, slab, pltpu.VMEM((4, N_GP, LANES), F32)],
        compiler_params=_params(1, 56),
        name="s5_scan",
    )(u, u, h0, lam_bar, b_pad, c_pad)


def _s5_discretise(lam_re, lam_im, log_step, b_re, b_im, c_re, c_im):
    delta = jnp.exp(log_step)[..., None]
    mag = jnp.exp(lam_re * delta)
    lbr = mag * jnp.cos(lam_im * delta)
    lbi = mag * jnp.sin(lam_im * delta)
    den = lam_re * lam_re + lam_im * lam_im
    cr = ((lbr - 1.0) * lam_re + lbi * lam_im) / den
    ci = (lbi * lam_re - (lbr - 1.0) * lam_im) / den
    bbr = cr[..., None] * b_re - ci[..., None] * b_im
    bbi = cr[..., None] * b_im + ci[..., None] * b_re
    lam_bar = jnp.stack([lbr, lbi], axis=2).reshape(DEPTH, 2, 2, N_GP, LANES)

    eye2 = jnp.eye(2, dtype=F32)
    slot = jax.nn.one_hot(jnp.arange(N_GP) % GP_PER_CHUNK, GP_PER_CHUNK, dtype=F32)
    bb = jnp.stack([bbr, bbi], axis=0).reshape(2, DEPTH, 2, N_GP, 2, SSM_STATE, SSM_GROUP_CH)
    b_small = jnp.einsum('cldgjph,jk->ldgjhckp', bb, eye2).reshape(DEPTH, 2, N_GP, 2 * SSM_GROUP_CH, 2 * LANES)
    b_pad = jnp.einsum('ldgrc,gs->ldgsrc', b_small, slot).reshape(DEPTH, 2, N_GP, LANES, 2 * LANES)
    cc = jnp.stack([c_re, -c_im], axis=0).reshape(2, DEPTH, 2, N_GP, 2, SSM_GROUP_CH, SSM_STATE)
    c_small = jnp.einsum('cldgjhp,jk->ldgckpjh', cc, eye2).reshape(DEPTH, 2, N_GP, 2 * LANES, 2 * SSM_GROUP_CH)
    c_pad = jnp.einsum('ldgrc,gs->ldgrsc', c_small, slot).reshape(DEPTH, 2, N_GP, 2 * LANES, LANES)
    return lam_bar, b_pad.astype(BF16), c_pad.astype(BF16)


def _outproj_kernel(*refs, moe):
    (attn_p_ref, attn_s_ref, yf_ref, yb_ref, u_ref, xp_ref, xs_ref, g1_ref, sh2_ref, sc2_ref, d_ref, wglu_ref,
     bglu_ref, wout_ref, lng_ref, lnb_ref) = refs[:16]
    if moe:
        router_ref, x1_ref, h2_ref, meta_ref, meta_t_ref, cnt_ref, wglu_b, wout_b, cnt = refs[16:]
    else:
        x1_ref, h2_ref, wglu_b, wout_b = refs[16:]
    attn = _group_rows(attn_p_ref, attn_s_ref)
    x = _group_rows(xp_ref, xs_ref)

    @pl.when(pl.program_id(0) == 0)
    def _():
        wglu_b[...] = wglu_ref[...].astype(BF16)
        wout_b[...] = wout_ref[...].astype(BF16)

    y = yf_ref[...] + yb_ref[...] + d_ref[...] * u_ref[...]
    g = 0.5 * y * (1.0 + jnp.tanh(math.sqrt(2.0 / math.pi) * (y + 0.044715 * (y * y * y))))
    z = jnp.dot(g.astype(BF16), wglu_b[...], preferred_element_type=F32) + bglu_ref[...]
    y_ssm = g * _sigmoid(z)
    mix = (jnp.dot(attn.astype(BF16), wout_b[:ATTN_WIDTH, :], preferred_element_type=F32)
           + jnp.dot(y_ssm.astype(BF16), wout_b[ATTN_WIDTH:, :], preferred_element_type=F32))
    x1 = _layer_norm(DEEPNORM_ALPHA * x + g1_ref[0] * mix) * lng_ref[...] + lnb_ref[...]
    x1_ref[...] = x1
    h2 = _layer_norm(x1) * (1.0 + sc2_ref[0]) + sh2_ref[0]
    h2_ref[...] = h2

    if moe:
        @pl.when(pl.program_id(0) == 0)
        def _():
            cnt[...] = jnp.zeros_like(cnt)

        router = router_ref[...]
        h_hi, r_hi = h2.astype(BF16), router.astype(BF16)
        h_lo = (h2 - h_hi.astype(F32)).astype(BF16)
        r_lo = (router - r_hi.astype(F32)).astype(BF16)
        logits = (jnp.dot(h_hi, r_hi, preferred_element_type=F32)
                  + (jnp.dot(h_hi, r_lo, preferred_element_type=F32)
                     + jnp.dot(h_lo, r_hi, preferred_element_type=F32)))
        tm = logits.shape[0]
        lane = lax.broadcasted_iota(jnp.int32, logits.shape, 1).astype(F32)
        neg = jnp.float32(-jnp.inf)
        l1 = jnp.where(lane < N_EXPERTS, logits, neg)
        m1 = jnp.max(l1, axis=-1, keepdims=True)
        i1 = jnp.min(jnp.where(l1 == m1, lane, float(LANES)), axis=-1, keepdims=True)
        l2 = jnp.where(lane == i1, neg, l1)
        m2 = jnp.max(l2, axis=-1, keepdims=True)
        i2 = jnp.min(jnp.where(l2 == m2, lane, float(LANES)), axis=-1, keepdims=True)
        e2 = jnp.exp(m2 - m1)
        w1 = 1.0 / (1.0 + e2)
        w2 = e2 / (1.0 + e2)
        hit = jnp.where((lane == i1) | (lane == i2), 1.0, 0.0)
        r = lax.broadcasted_iota(jnp.int32, (tm, tm), 0)
        c = lax.broadcasted_iota(jnp.int32, (tm, tm), 1)
        earlier = jnp.where(c < r, 1.0, 0.0).astype(BF16)
        rank = cnt[...] + jnp.dot(earlier, hit.astype(BF16), preferred_element_type=F32)
        r1 = jnp.sum(jnp.where(lane == i1, rank, 0.0), axis=-1, keepdims=True)
        r2 = jnp.sum(jnp.where(lane == i2, rank, 0.0), axis=-1, keepdims=True)
        cnt[...] = cnt[...] + jnp.sum(hit, axis=0, keepdims=True)
        cnt_ref[...] = cnt[...]
        fields = (i1, i2, r1, r2, w1, w2)
        meta = jnp.zeros_like(logits)
        for k, f in enumerate(fields):
            meta = jnp.where(lane == float(k), f, meta)
        meta_ref[...] = meta
        meta_t_ref[...] = meta.T[:meta_t_ref.shape[0], :]


def _out_projection(layer, attn_p, attn_s, yf, yb, u, x_pair, mod3, d_skip, w_glu, b_glu, w_out, ln_g, ln_b,
                    router):
    tm = TM_PROJ
    moe = router is not None
    row = lambda w: pl.BlockSpec((tm, w), lambda i: (i, 0))
    full = lambda a: pl.BlockSpec(a.shape, lambda i: (0,) * a.ndim)
    args = [attn_p, attn_s, yf, yb, u, *x_pair, mod3, mod3, mod3, d_skip, w_glu, b_glu, w_out, ln_g, ln_b]
    in_specs = (_group_specs(tm, ATTN_WIDTH, False)
                + [row(SSM_WIDTH), row(SSM_WIDTH), row(SSM_WIDTH)]
                + _group_specs(tm, D_MODEL, x_pair[0] is x_pair[1])
                + [_mod_spec(layer, 2, tm), _mod_spec(layer, 3, tm), _mod_spec(layer, 4, tm),
                   full(d_skip), _layer_spec(w_glu, layer), full(b_glu), _layer_spec(w_out, layer),
                   full(ln_g), full(ln_b)])
    out_specs = [row(D_MODEL), row(D_MODEL)]
    out_shape = [jax.ShapeDtypeStruct((ROWS, D_MODEL), F32), jax.ShapeDtypeStruct((ROWS, D_MODEL), F32)]
    scratch = [pltpu.VMEM((SSM_WIDTH, SSM_WIDTH), BF16), pltpu.VMEM((D_MODEL, D_MODEL), BF16)]
    if moe:
        args.append(router)
        in_specs.append(_layer_spec(router, layer // 2))
        out_specs += [row(LANES), pl.BlockSpec((META_ROWS, tm), lambda i: (0, i)),
                      pl.BlockSpec((1, LANES), lambda i: (0, 0))]
        out_shape += [jax.ShapeDtypeStruct((ROWS, LANES), F32), jax.ShapeDtypeStruct((META_ROWS, ROWS), F32),
                      jax.ShapeDtypeStruct((1, LANES), F32)]
        scratch.append(pltpu.VMEM((1, LANES), F32))
    return pl.pallas_call(
        functools.partial(_outproj_kernel, moe=moe),
        grid=(ROWS // tm,),
        in_specs=in_specs,
        out_specs=out_specs,
        out_shape=out_shape,
        scratch_shapes=scratch,
        compiler_params=_params(1, 48),
        name="out_projection",
    )(*args)


def _swiglu_partial(xb, wg, wu, wd):
    a = jnp.dot(xb, wg, preferred_element_type=F32)
    b = jnp.dot(xb, wu, preferred_element_type=F32)
    act = a * _sigmoid(a) * b
    return act, lambda act_: jnp.dot(act_.astype(BF16), wd, preferred_element_type=F32)


def _ffn_kernel(h_ref, x1_ref, g2_ref, wg_ref, wu_ref, wd_ref, lng_ref, lnb_ref, o_ref, hb, acc, *, tf, d_ff):
    j = pl.program_id(1)

    @pl.when(j == 0)
    def _():
        hb[...] = h_ref[...].astype(BF16)
        acc[...] = jnp.zeros_like(acc)

    last = pl.num_programs(1) - 1
    tail = d_ff % tf

    def block(width, valid):
        wd = wd_ref[:width, :]
        if valid < width:
            r = lax.broadcasted_iota(jnp.int32, wd.shape, 0)
            wd = jnp.where(r < valid, wd, 0.0)
        act, down = _swiglu_partial(hb[...], wg_ref[:, :width].astype(BF16), wu_ref[:, :width].astype(BF16),
                                    wd.astype(BF16))
        if valid < width:
            col = lax.broadcasted_iota(jnp.int32, act.shape, 1)
            act = jnp.where(col < valid, act, 0.0)
        acc[...] += down(act)

    if tail:
        @pl.when(j < last)
        def _():
            block(tf, tf)

        @pl.when(j == last)
        def _():
            block(-(-tail // LANES) * LANES, tail)
    else:
        block(tf, tf)

    @pl.when(j == last)
    def _():
        y = DEEPNORM_ALPHA * x1_ref[...] + g2_ref[0] * acc[...]
        o_ref[...] = _layer_norm(y) * lng_ref[...] + lnb_ref[...]


def _ffn(layer, h2, x1, mod3, wg, wu, wd, ln_g, ln_b):
    tm, tf, d_ff = TM_FFN, TF_DENSE, D_FF
    w_idx = layer // 2
    up = pl.BlockSpec((None, D_MODEL, tf), lambda i, j: (w_idx, 0, j))
    down = pl.BlockSpec((None, tf, D_MODEL), lambda i, j: (w_idx, j, 0))
    row = lambda w: pl.BlockSpec((tm, w), lambda i, j: (i, 0))
    vec = pl.BlockSpec((1, D_MODEL), lambda i, j: (0, 0))
    g2 = pl.BlockSpec((1, 1, D_MODEL), lambda i, j: (layer * N_MOD + _mod_row(i, tm), 0, 5))
    return pl.pallas_call(
        functools.partial(_ffn_kernel, tf=tf, d_ff=d_ff),
        grid=(ROWS // tm, pl.cdiv(d_ff, tf)),
        in_specs=[row(D_MODEL), row(D_MODEL), g2, up, up, down, vec, vec],
        out_specs=row(D_MODEL),
        out_shape=jax.ShapeDtypeStruct((ROWS, D_MODEL), F32),
        scratch_shapes=[pltpu.VMEM((tm, D_MODEL), BF16), pltpu.VMEM((tm, D_MODEL), F32)],
        compiler_params=_params(2, 48),
        name="ffn_dense",
    )(h2, x1, mod3, wg, wu, wd, ln_g, ln_b)


def _route_plan(meta_t, counts):
    i32 = jnp.int32
    e1, e2 = meta_t[0].astype(i32), meta_t[1].astype(i32)
    r1, r2 = meta_t[2].astype(i32), meta_t[3].astype(i32)
    cnt = counts[0, :N_EXPERTS].astype(i32)
    tiles = (cnt + MOE_TILE - 1) // MOE_TILE
    tile_end = jnp.cumsum(tiles)
    tile_start = tile_end - tiles
    base = tile_start * MOE_TILE
    pos1 = (base[e1] + r1) * ROW_TILE
    pos2 = (base[e2] + r2) * ROW_TILE
    t = jnp.arange(MOE_TILES, dtype=i32)
    n_used = tile_end[-1]
    used = t < n_used
    owner = jnp.sum((t[:, None] >= tile_end[None, :]).astype(i32), axis=1)
    owner = jnp.minimum(owner, N_EXPERTS - 1)
    rows = jnp.clip(cnt[owner] - (t - tile_start[owner]) * MOE_TILE, 0, MOE_TILE)
    rows = jnp.where(used, rows, 0)
    n_sub = (rows + MOE_SUB - 1) // MOE_SUB
    last_owner = owner[jnp.maximum(n_used - 1, 0)]
    tile_expert = jnp.where(used, owner, last_owner)
    zero_fill = (rows < MOE_TILE).astype(i32)
    return pos1, pos2, tile_expert, n_sub, zero_fill


def _slot_rows(start):
    return pl.ds(pl.multiple_of(start, ROW_TILE), ROW_TILE)


def _dispatch_kernel(zf_ref, p1_ref, p2_ref, h_ref, xs_ref, src, zbuf, zsem, sem):
    tm = h_ref.shape[0]
    tile_rows = MOE_TILE * ROW_TILE

    def zero_copy(k):
        return pltpu.make_async_copy(zbuf, xs_ref.at[pl.ds(k * tile_rows, tile_rows)], zsem)

    @pl.when(pl.program_id(0) == 0)
    def _():
        zbuf[...] = jnp.zeros_like(zbuf)
        for k in range(MOE_TILES):
            @pl.when(zf_ref[k] != 0)
            def _():
                zero_copy(k).start()
        for k in range(MOE_TILES):
            @pl.when(zf_ref[k] != 0)
            def _():
                zero_copy(k).wait()

    for s in range(ROW_TILE):
        src[pl.ds(s, tm, stride=ROW_TILE), :] = h_ref[:, s * LANES:(s + 1) * LANES]

    def row_copy(r, p_ref):
        return pltpu.make_async_copy(src.at[_slot_rows(r * ROW_TILE)], xs_ref.at[_slot_rows(p_ref[r])], sem)

    def start(r, carry):
        row_copy(r, p1_ref).start()
        row_copy(r, p2_ref).start()
        return carry

    def wait(r, carry):
        row_copy(r, p1_ref).wait()
        row_copy(r, p2_ref).wait()
        return carry

    lax.fori_loop(0, tm, start, 0, unroll=DMA_UNROLL)
    lax.fori_loop(0, tm, wait, 0, unroll=DMA_UNROLL)


def _moe_dispatch(h2, pos1, pos2, zero_fill):
    tm = TM_ROUTE
    pos = pl.BlockSpec((tm,), lambda i, zf: (i,), memory_space=pltpu.SMEM)
    return pl.pallas_call(
        _dispatch_kernel,
        grid_spec=pltpu.PrefetchScalarGridSpec(
            num_scalar_prefetch=1,
            grid=(ROWS // tm,),
            in_specs=[pos, pos, pl.BlockSpec((tm, D_MODEL), lambda i, zf: (i, 0))],
            out_specs=pl.BlockSpec(memory_space=pl.ANY),
            scratch_shapes=[pltpu.VMEM((tm * ROW_TILE, LANES), F32),
                            pltpu.VMEM((MOE_TILE * ROW_TILE, LANES), F32),
                            pltpu.SemaphoreType.DMA(()), pltpu.SemaphoreType.DMA(())]),
        out_shape=jax.ShapeDtypeStruct((MOE_ROWS * ROW_TILE, LANES), F32),
        compiler_params=_params(1, 32),
        name="moe_dispatch",
    )(zero_fill, pos1, pos2, h2)


def _expert_kernel(te_ref, ns_ref, xs_ref, wg_ref, wu_ref, wd_ref, ys_ref, xb, acc):
    i, j = pl.program_id(0), pl.program_id(1)
    n_sub = ns_ref[i]

    @pl.when(j == 0)
    def _():
        for s in range(ROW_TILE):
            xb[:, s * LANES:(s + 1) * LANES] = xs_ref[pl.ds(s, MOE_TILE, stride=ROW_TILE), :].astype(BF16)
        acc[...] = jnp.zeros_like(acc)

    def accumulate(rows):
        act, down = _swiglu_partial(xb[rows, :], wg_ref[...].astype(BF16), wu_ref[...].astype(BF16),
                                    wd_ref[...].astype(BF16))
        acc[rows, :] += down(act)

    full = MOE_TILE // MOE_SUB

    @pl.when(n_sub == full)
    def _():
        accumulate(slice(None))

    @pl.when((n_sub > 0) & (n_sub < full))
    def _():
        def body(s, carry):
            accumulate(pl.ds(pl.multiple_of(s * MOE_SUB, MOE_SUB), MOE_SUB))
            return carry
        lax.fori_loop(0, n_sub, body, 0)

    @pl.when(j == pl.num_programs(1) - 1)
    def _():
        for s in range(ROW_TILE):
            ys_ref[pl.ds(s, MOE_TILE, stride=ROW_TILE), :] = acc[:, s * LANES:(s + 1) * LANES]


def _moe_experts(layer, xs, tile_expert, n_sub, wg, wu, wd):
    tf = TF_MOE
    n_j = EXPERT_FF // tf
    w_idx = layer // 2
    ff = lambda i, j, ns: jnp.where(ns[i] > 0, j, n_j - 1)
    up = pl.BlockSpec((None, None, D_MODEL, tf), lambda i, j, te, ns: (w_idx, te[i], 0, ff(i, j, ns)))
    down = pl.BlockSpec((None, None, tf, D_MODEL), lambda i, j, te, ns: (w_idx, te[i], ff(i, j, ns), 0))
    rows = pl.BlockSpec((MOE_TILE * ROW_TILE, LANES), lambda i, j, te, ns: (i, 0))
    return pl.pallas_call(
        _expert_kernel,
        grid_spec=pltpu.PrefetchScalarGridSpec(
            num_scalar_prefetch=2,
            grid=(MOE_TILES, n_j),
            in_specs=[rows, up, up, down],
            out_specs=rows,
            scratch_shapes=[pltpu.VMEM((MOE_TILE, D_MODEL), BF16), pltpu.VMEM((MOE_TILE, D_MODEL), F32)]),
        out_shape=jax.ShapeDtypeStruct((MOE_ROWS * ROW_TILE, LANES), F32),
        compiler_params=_params(2, 48),
        name="moe_experts",
    )(tile_expert, n_sub, xs, wg, wu, wd)


def _combine_kernel(p1_ref, p2_ref, ys_ref, x1_ref, meta_ref, g2_ref, lng_ref, lnb_ref, *rest, split):
    b1, b2, f, sem = rest[-4:]
    tm = x1_ref.shape[0]

    def row_copy(r, p_ref, buf):
        return pltpu.make_async_copy(ys_ref.at[_slot_rows(p_ref[r])], buf.at[_slot_rows(r * ROW_TILE)], sem)

    def start(r, carry):
        row_copy(r, p1_ref, b1).start()
        row_copy(r, p2_ref, b2).start()
        return carry

    def wait(r, carry):
        row_copy(r, p1_ref, b1).wait()
        row_copy(r, p2_ref, b2).wait()
        return carry

    lax.fori_loop(0, tm, start, 0, unroll=DMA_UNROLL)
    lax.fori_loop(0, tm, wait, 0, unroll=DMA_UNROLL)
    meta = meta_ref[...]
    w1, w2 = meta[:, 4:5], meta[:, 5:6]
    for s in range(ROW_TILE):
        rows = pl.ds(s, tm, stride=ROW_TILE)
        f[:, s * LANES:(s + 1) * LANES] = w1 * b1[rows, :] + w2 * b2[rows, :]
    y = DEEPNORM_ALPHA * x1_ref[...] + g2_ref[0] * f[...]
    out = _layer_norm(y) * lng_ref[...] + lnb_ref[...]
    if split:
        op_ref, os_ref = rest[:2]
        is_prompt = pl.program_id(0) < P_ROWS // tm

        @pl.when(is_prompt)
        def _():
            op_ref[...] = out

        @pl.when(jnp.logical_not(is_prompt))
        def _():
            os_ref[...] = out
    else:
        rest[0][...] = out


def _moe_combine(layer, ys, pos1, pos2, x1, meta, mod3, ln_g, ln_b, split):
    tm = TM_COMBINE
    pos = pl.BlockSpec((tm,), lambda i: (i,), memory_space=pltpu.SMEM)
    row = lambda w: pl.BlockSpec((tm, w), lambda i: (i, 0))
    vec = pl.BlockSpec((1, D_MODEL), lambda i: (0, 0))
    if split:
        out_specs = _group_specs(tm, D_MODEL, False)
        out_shape = [jax.ShapeDtypeStruct((P_ROWS, D_MODEL), F32), jax.ShapeDtypeStruct((S_ROWS, D_MODEL), F32)]
    else:
        out_specs = row(D_MODEL)
        out_shape = jax.ShapeDtypeStruct((ROWS, D_MODEL), F32)
    return pl.pallas_call(
        functools.partial(_combine_kernel, split=split),
        grid=(ROWS // tm,),
        in_specs=[pos, pos, pl.BlockSpec(memory_space=pl.ANY), row(D_MODEL), row(LANES),
                  _mod_spec(layer, 5, tm), vec, vec],
        out_specs=out_specs,
        out_shape=out_shape,
        scratch_shapes=[pltpu.VMEM((tm * ROW_TILE, LANES), F32), pltpu.VMEM((tm * ROW_TILE, LANES), F32),
                        pltpu.VMEM((tm, D_MODEL), F32), pltpu.SemaphoreType.DMA(())],
        compiler_params=_params(1, 32),
        name="moe_combine",
    )(pos1, pos2, ys, x1, meta, mod3, ln_g, ln_b)


def _rope_tables():
    rows = DEC_SEQ // GRID_W
    row = jnp.repeat(jnp.arange(rows), GRID_W).astype(F32)
    col = jnp.tile(jnp.arange(GRID_W), rows).astype(F32)
    freqs = ROPE_THETA ** (-jnp.arange(ROPE_PAIRS, dtype=F32) / ROPE_PAIRS)
    ang = jnp.stack([row[:, None] * freqs, col[:, None] * freqs], axis=1)
    cos, sin = jnp.cos(ang), jnp.sin(ang)
    cos_h = jnp.concatenate([cos[:, 0], cos[:, 0], cos[:, 1], cos[:, 1]], axis=-1)
    sin_h = jnp.concatenate([-sin[:, 0], sin[:, 0], -sin[:, 1], sin[:, 1]], axis=-1)
    per_tile = LANES // HEAD_DIM
    cos_t = jnp.concatenate([jnp.ones((ROPE_ID_ROWS, LANES), F32), jnp.tile(cos_h, (1, per_tile))], axis=0)
    sin_t = jnp.concatenate([jnp.zeros((ROPE_ID_ROWS, LANES), F32), jnp.tile(sin_h, (1, per_tile))], axis=0)
    return cos_t, sin_t


def kernel(x_prompt, x_sample, c, cache_k, cache_v, state_ssm_re, state_ssm_im, c_ctx, w_ada, b_ada,
           w_in, q_gain, k_gain, ssm_lambda_re, ssm_lambda_im, ssm_log_step, ssm_b_re, ssm_b_im,
           ssm_c_re, ssm_c_im, ssm_d, w_glu, b_glu, w_out, ln1_g, ln1_b, ln2_g, ln2_b,
           ffn_w_gate, ffn_w_up, ffn_w_down, router_w, moe_w_gate, moe_w_up, moe_w_down):
    x_pair = (x_prompt.reshape(P_ROWS, D_MODEL), x_sample.reshape(S_ROWS, D_MODEL))
    cvec = jnp.concatenate([c_ctx[None], c, jnp.zeros((N_MOD - 1 - DEC_BATCH, D_MODEL), F32)], axis=0)
    mod3 = _modulation(cvec, w_ada, b_ada).reshape(DEPTH * N_MOD, 1, 6 * D_MODEL)
    rope_cos, rope_sin = _rope_tables()
    lam_bar, b_pad, c_pad = _s5_discretise(ssm_lambda_re, ssm_lambda_im, ssm_log_step,
                                           ssm_b_re, ssm_b_im, ssm_c_re, ssm_c_im)
    cache_k4 = cache_k.reshape(DEC_BATCH, DEPTH, PAST_LEN, KV_WIDTH)
    cache_v4 = cache_v.reshape(DEC_BATCH, DEPTH, PAST_LEN, KV_WIDTH)
    h0_s = jnp.stack([state_ssm_re, state_ssm_im], axis=3)
    h0_s = h0_s.reshape(DEC_BATCH, DEPTH, 4, N_GP, LANES)
    h0_all = jnp.concatenate([jnp.zeros((BATCH, DEPTH, 4, N_GP, LANES), F32), h0_s], axis=0)
    router_pad = jnp.pad(router_w, ((0, 0), (0, 0), (0, LANES - N_EXPERTS)))
    gain2 = lambda g: jnp.tile(g, (1, LANES // HEAD_DIM))

    new_k, new_v, new_re, new_im = [], [], [], []
    for l in range(DEPTH):
        q, k_rot, k_norm, v, u = _in_projection(l, x_pair, mod3, w_in, gain2(q_gain[l:l + 1]),
                                                gain2(k_gain[l:l + 1]), rope_cos, rope_sin)
        attn_p, attn_s = _attention(l, q, k_rot, v, cache_k4, cache_v4)
        yf, yb, fin = _s5_scan(l, u, h0_all, lam_bar, b_pad, c_pad)
        moe = l % 2 == 1
        outs = _out_projection(l, attn_p, attn_s, yf, yb, u, x_pair, mod3, ssm_d[l:l + 1], w_glu,
                               b_glu[l:l + 1], w_out, ln1_g[l:l + 1], ln1_b[l:l + 1],
                               router_pad if moe else None)
        if moe:
            x1, h2, meta, meta_t, counts = outs
            pos1, pos2, tile_expert, n_sub, zero_fill = _route_plan(meta_t, counts)
            xs = _moe_dispatch(h2, pos1, pos2, zero_fill)
            ys = _moe_experts(l, xs, tile_expert, n_sub, moe_w_gate, moe_w_up, moe_w_down)
            last = l == DEPTH - 1
            x = _moe_combine(l, ys, pos1, pos2, x1, meta, mod3, ln2_g[l:l + 1], ln2_b[l:l + 1], split=last)
            x_pair = tuple(x) if last else (x, x)
        else:
            x1, h2 = outs
            x = _ffn(l, h2, x1, mod3, ffn_w_gate, ffn_w_up, ffn_w_down, ln2_g[l:l + 1], ln2_b[l:l + 1])
            x_pair = (x, x)
        new_k.append(k_norm[:P_ROWS].reshape(BATCH, SEQ, N_KV_HEADS, HEAD_DIM))
        new_v.append(v[:P_ROWS].reshape(BATCH, SEQ, N_KV_HEADS, HEAD_DIM))
        fin_p = fin[:BATCH].reshape(BATCH, 2, 2, SSM_GROUPS, SSM_STATE)
        new_re.append(fin_p[:, :, 0])
        new_im.append(fin_p[:, :, 1])

    assert DEPTH % 2 == 0, "the last layer is the routed one and returns the two groups separately"
    y_prompt = x_pair[0].reshape(BATCH, SEQ, D_MODEL)
    y_sample = x_pair[1].reshape(DEC_BATCH, DEC_SEQ, D_MODEL)
    return (y_prompt, y_sample, jnp.stack(new_k, axis=1), jnp.stack(new_v, axis=1),
            jnp.stack(new_re, axis=1), jnp.stack(new_im, axis=1))
```

```python
import functools
import math

import jax
import jax.numpy as jnp
from jax import lax
from jax.experimental import pallas as pl
from jax.experimental.pallas import tpu as pltpu

D_MODEL = 1024
BATCH = 16
SEQ = 256
DEPTH = 4
DEC_BATCH = 2
DEC_SEQ = 2048
PAST_LEN = 256
GRID_W = 64
N_HEADS = 8
N_KV_HEADS = 2
HEAD_DIM = 64
ATTN_WIDTH = N_HEADS * HEAD_DIM
KV_WIDTH = N_KV_HEADS * HEAD_DIM
SSM_WIDTH = D_MODEL - ATTN_WIDTH
SSM_GROUP_CH = 16
SSM_GROUPS = SSM_WIDTH // SSM_GROUP_CH
SSM_STATE = 64
IN_WIDTH = ATTN_WIDTH + 2 * KV_WIDTH + SSM_WIDTH
ROPE_THETA = 10000.0
ROPE_PAIRS = HEAD_DIM // 4
D_FF = 2752
N_EXPERTS = 8
TOP_K = 2
EXPERT_FF = 3584
DEEPNORM_ALPHA = (2 * DEPTH) ** 0.25
NORM_EPS = 1e-6

F32 = jnp.float32
BF16 = jnp.bfloat16
HIGHEST = lax.Precision.HIGHEST
LANES = 128

P_ROWS = BATCH * SEQ
S_ROWS = DEC_BATCH * DEC_SEQ
ROWS = P_ROWS + S_ROWS
N_MOD = 8
GRP = N_HEADS // N_KV_HEADS
SCALE = HEAD_DIM ** -0.5

TM_PROJ = 512
ROPE_ID_ROWS = TM_PROJ
TQ_PROMPT = SEQ
TQ_SAMPLE = 128
SCAN_CHUNK = 256
SCAN_PITCH = SCAN_CHUNK + 4
N_GP = SSM_GROUPS // 2
GP_PER_CHUNK = LANES // (2 * SSM_GROUP_CH)
S_CHUNKS = DEC_SEQ // SCAN_CHUNK
N_SCAN_ITEMS = BATCH * (SEQ // SCAN_CHUNK) + DEC_BATCH * S_CHUNKS
TM_FFN = 1024
TF_DENSE = 512
TF_MOE = 512
MOE_TILE = 1024
MOE_SUB = 256
MOE_TILES = TOP_K * ROWS // MOE_TILE + N_EXPERTS
MOE_ROWS = MOE_TILES * MOE_TILE
TM_ROUTE = 1024
TM_COMBINE = 512
META_ROWS = 8
ROW_TILE = 8
DMA_UNROLL = 8
MIB = 1024 * 1024


def _params(n_axes, vmem_mib):
    return pltpu.CompilerParams(dimension_semantics=("arbitrary",) * n_axes,
                                vmem_limit_bytes=vmem_mib * MIB)


def _layer_norm(x):
    mu = jnp.mean(x, axis=-1, keepdims=True)
    xc = x - mu
    var = jnp.mean(xc * xc, axis=-1, keepdims=True)
    return xc * lax.rsqrt(var + NORM_EPS)


def _sigmoid(x):
    return 1.0 / (1.0 + jnp.exp(-x))


def _mod_row(tile, tm):
    p_tiles = P_ROWS // tm
    return jnp.where(tile < p_tiles, 0, 1 + (tile - p_tiles) // (DEC_SEQ // tm))


def _mod_kernel(c_ref, w_ref, b_ref, o_ref):
    c = c_ref[...]
    a = c * _sigmoid(c)
    o_ref[0] = jnp.dot(a, w_ref[0], precision=HIGHEST, preferred_element_type=F32) + b_ref[0]


def _modulation(cvec, w_ada, b_ada):
    tn = 3072
    return pl.pallas_call(
        _mod_kernel,
        grid=(DEPTH, 6 * D_MODEL // tn),
        in_specs=[pl.BlockSpec((N_MOD, D_MODEL), lambda l, j: (0, 0)),
                  pl.BlockSpec((1, D_MODEL, tn), lambda l, j: (l, 0, j)),
                  pl.BlockSpec((1, 1, tn), lambda l, j: (l, 0, j))],
        out_specs=pl.BlockSpec((1, N_MOD, tn), lambda l, j: (l, 0, j)),
        out_shape=jax.ShapeDtypeStruct((DEPTH, N_MOD, 6 * D_MODEL), F32),
        compiler_params=_params(2, 40),
        name="adaln_mod",
    )(cvec, w_ada, b_ada.reshape(DEPTH, 1, 6 * D_MODEL))


def _layer_spec(stacked, index):
    zeros = (0,) * (stacked.ndim - 1)
    return pl.BlockSpec((None,) + stacked.shape[1:], lambda i: (index,) + zeros)


def _mod_spec(layer, part, tm):
    return pl.BlockSpec((1, 1, D_MODEL), lambda i: (layer * N_MOD + _mod_row(i, tm), 0, part))


def _group_rows(p_ref, s_ref):
    is_prompt = pl.program_id(0) < P_ROWS // p_ref.shape[0]
    return jnp.where(is_prompt, p_ref[...], s_ref[...])


def _group_specs(tm, width, stacked):
    p_tiles = P_ROWS // tm
    offset = p_tiles if stacked else 0
    return [pl.BlockSpec((tm, width), lambda i: (jnp.minimum(i, p_tiles - 1), 0)),
            pl.BlockSpec((tm, width), lambda i: (jnp.maximum(i - p_tiles, 0) + offset, 0))]


def _inproj_kernel(xp_ref, xs_ref, sh_ref, sc_ref, w_ref, qg_ref, kg_ref, cos_ref, sin_ref,
                   q_ref, kr_ref, kn_ref, v_ref, u_ref, wb_ref):
    @pl.when(pl.program_id(0) == 0)
    def _():
        wb_ref[...] = w_ref[...].astype(BF16)

    h = _layer_norm(_group_rows(xp_ref, xs_ref)) * (1.0 + sc_ref[0]) + sh_ref[0]
    proj = jnp.dot(h.astype(BF16), wb_ref[...], preferred_element_type=F32)

    tm = proj.shape[0]
    head_of = lambda idx: lax.shift_right_logical(idx, int(math.log2(HEAD_DIM)))
    r = head_of(lax.broadcasted_iota(jnp.int32, (LANES, LANES), 0))
    c = head_of(lax.broadcasted_iota(jnp.int32, (LANES, LANES), 1))
    seg = jnp.where(r == c, 1.0 / HEAD_DIM, 0.0).astype(BF16)
    lane = lax.broadcasted_iota(jnp.int32, (tm, LANES), 1)
    first_half = (lane & (2 * ROPE_PAIRS - 1)) < ROPE_PAIRS
    cos = cos_ref[...]
    sin = sin_ref[...]

    def head_norm(t, gain):
        sq = t * t
        hi = sq.astype(BF16)
        lo = (sq - hi.astype(F32)).astype(BF16)
        ms = (jnp.dot(hi, seg, preferred_element_type=F32) + jnp.dot(lo, seg, preferred_element_type=F32))
        return t * lax.rsqrt(ms + NORM_EPS) * gain

    def rope(t):
        partner = jnp.where(first_half, pltpu.roll(t, LANES - ROPE_PAIRS, 1), pltpu.roll(t, ROPE_PAIRS, 1))
        return t * cos + partner * sin

    for j in range(ATTN_WIDTH // LANES):
        sl = slice(j * LANES, (j + 1) * LANES)
        q_ref[:, sl] = rope(head_norm(proj[:, sl], qg_ref[...])) * SCALE
    k = head_norm(proj[:, ATTN_WIDTH:ATTN_WIDTH + KV_WIDTH], kg_ref[...])
    kn_ref[...] = k
    kr_ref[...] = rope(k)
    v_ref[...] = proj[:, ATTN_WIDTH + KV_WIDTH:ATTN_WIDTH + 2 * KV_WIDTH]
    u_ref[...] = proj[:, ATTN_WIDTH + 2 * KV_WIDTH:]


def _in_projection(layer, x_pair, mod3, w_in, q_gain, k_gain, rope_cos, rope_sin):
    tm = TM_PROJ
    stacked = x_pair[0] is x_pair[1]
    p_tiles = P_ROWS // tm
    t_tiles = DEC_SEQ // tm

    def rope_idx(i):
        return (jnp.where(i < p_tiles, 0, 1 + (i - p_tiles) % t_tiles), 0)

    row = lambda w: pl.BlockSpec((tm, w), lambda i: (i, 0))
    full = lambda a: pl.BlockSpec(a.shape, lambda i: (0,) * a.ndim)
    return pl.pallas_call(
        _inproj_kernel,
        grid=(ROWS // tm,),
        in_specs=_group_specs(tm, D_MODEL, stacked) + [
                  _mod_spec(layer, 0, tm), _mod_spec(layer, 1, tm), _layer_spec(w_in, layer),
                  full(q_gain), full(k_gain),
                  pl.BlockSpec((tm, LANES), rope_idx), pl.BlockSpec((tm, LANES), rope_idx)],
        out_specs=[row(ATTN_WIDTH), row(KV_WIDTH), row(KV_WIDTH), row(KV_WIDTH), row(SSM_WIDTH)],
        out_shape=[jax.ShapeDtypeStruct((ROWS, ATTN_WIDTH), F32),
                   jax.ShapeDtypeStruct((ROWS, KV_WIDTH), F32),
                   jax.ShapeDtypeStruct((ROWS, KV_WIDTH), F32),
                   jax.ShapeDtypeStruct((ROWS, KV_WIDTH), F32),
                   jax.ShapeDtypeStruct((ROWS, SSM_WIDTH), F32)],
        scratch_shapes=[pltpu.VMEM((D_MODEL, IN_WIDTH), BF16)],
        compiler_params=_params(1, 48),
        name="in_projection",
    )(*x_pair, mod3, mod3, w_in, q_gain, k_gain, rope_cos, rope_sin)


def _attn_kernel(*refs, tq, has_ctx):
    if has_ctx:
        q_ref, k_ref, v_ref, kc_ref, vc_ref, o_ref = refs
    else:
        q_ref, k_ref, v_ref, o_ref = refs
    nt = (((1,), (1,)), ((), ()))
    for kv in range(N_KV_HEADS):
        sl = slice(kv * HEAD_DIM, (kv + 1) * HEAD_DIM)
        k = k_ref[:, sl].astype(BF16)
        v = v_ref[:, sl].astype(BF16)
        heads = [q_ref[:, (kv * GRP + g) * HEAD_DIM:(kv * GRP + g + 1) * HEAD_DIM] for g in range(GRP)]
        qs = jnp.concatenate(heads, axis=0).astype(BF16)
        s = lax.dot_general(qs, k, nt, preferred_element_type=F32)
        m = jnp.max(s, axis=-1, keepdims=True)
        if has_ctx:
            s_ctx = lax.dot_general(qs, kc_ref[:, sl].astype(BF16), nt, preferred_element_type=F32)
            m = jnp.maximum(m, jnp.max(s_ctx, axis=-1, keepdims=True))
        p = jnp.exp(s - m)
        den = jnp.sum(p, axis=-1, keepdims=True)
        o = jnp.dot(p.astype(BF16), v, preferred_element_type=F32)
        if has_ctx:
            p_ctx = jnp.exp(s_ctx - m)
            den = den + jnp.sum(p_ctx, axis=-1, keepdims=True)
            o = o + jnp.dot(p_ctx.astype(BF16), vc_ref[:, sl].astype(BF16), preferred_element_type=F32)
        o = o / den
        for g in range(GRP):
            h = kv * GRP + g
            o_ref[:, h * HEAD_DIM:(h + 1) * HEAD_DIM] = o[g * tq:(g + 1) * tq]


def _attention(layer, q, k_rot, v, cache_k4, cache_v4):
    tq = TQ_PROMPT
    blk = lambda w: pl.BlockSpec((tq, w), lambda b: (b, 0))
    attn_p = pl.pallas_call(
        functools.partial(_attn_kernel, tq=tq, has_ctx=False),
        grid=(BATCH,),
        in_specs=[blk(ATTN_WIDTH), blk(KV_WIDTH), blk(KV_WIDTH)],
        out_specs=blk(ATTN_WIDTH),
        out_shape=jax.ShapeDtypeStruct((P_ROWS, ATTN_WIDTH), F32),
        compiler_params=_params(1, 32),
        name="attn_prompt",
    )(q, k_rot, v)

    tq = TQ_SAMPLE
    q_tiles = DEC_SEQ // tq
    own = pl.BlockSpec((DEC_SEQ, KV_WIDTH), lambda b, j: (P_ROWS // DEC_SEQ + b, 0))
    ctx = pl.BlockSpec((None, None, PAST_LEN, KV_WIDTH), lambda b, j: (b, layer, 0, 0))
    attn_s = pl.pallas_call(
        functools.partial(_attn_kernel, tq=tq, has_ctx=True),
        grid=(DEC_BATCH, q_tiles),
        in_specs=[pl.BlockSpec((tq, ATTN_WIDTH), lambda b, j: (P_ROWS // tq + b * q_tiles + j, 0)),
                  own, own, ctx, ctx],
        out_specs=pl.BlockSpec((tq, ATTN_WIDTH), lambda b, j: (b * q_tiles + j, 0)),
        out_shape=jax.ShapeDtypeStruct((S_ROWS, ATTN_WIDTH), F32),
        compiler_params=_params(2, 48),
        name="attn_sample",
    )(q, k_rot, v, cache_k4, cache_v4)
    return attn_p, attn_s


def _scan_item(i):
    j = i - BATCH
    seq = jnp.where(i < BATCH, i, BATCH + j // S_CHUNKS)
    bwd = jnp.where(i < BATCH, i, BATCH + (j // S_CHUNKS) * S_CHUNKS + (S_CHUNKS - 1) - j % S_CHUNKS)
    return seq, i, bwd


def _s5_stages(uf_ref, ub_ref, lam_ref, b_ref, c_ref, yf_ref, yb_ref, h, drive_in, drive_out, slab_in, slab_out):
    tc, pitch = SCAN_CHUNK, SCAN_PITCH
    per_dot = tc // (2 * N_GP)

    def section(d, part):
        return (2 * d + part) * N_GP * pitch

    lam = [[lam_ref[d, part] for part in range(2)] for d in range(2)]
    u_bf16 = {}
    y_acc = {}
    h = list(h)
    for k in range(2 * N_GP):
        d, gp = divmod(k, N_GP)
        ch = gp // GP_PER_CHUNK
        u_ref, y_ref = (uf_ref, yf_ref) if d == 0 else (ub_ref, yb_ref)
        if (d, ch) not in u_bf16:
            u_bf16[d, ch] = u_ref[:, ch * LANES:(ch + 1) * LANES].astype(BF16)
        bu = jnp.dot(u_bf16[d, ch], b_ref[d, gp], preferred_element_type=F32)
        drive_out[pl.ds(section(d, 0) + gp * pitch, tc), :] = bu[:, :LANES]
        drive_out[pl.ds(section(d, 1) + gp * pitch, tc), :] = bu[:, LANES:]
        for t in range(k * per_dot, (k + 1) * per_dot):
            for dd in range(2):
                tt = t if dd == 0 else tc - 1 - t
                rows_re = pl.ds(section(dd, 0) + tt, N_GP, stride=pitch)
                rows_im = pl.ds(section(dd, 1) + tt, N_GP, stride=pitch)
                ar, ai = lam[dd]
                hr, hi = h[2 * dd], h[2 * dd + 1]
                nr = ar * hr - ai * hi + drive_in[rows_re, :]
                ni = ar * hi + ai * hr + drive_in[rows_im, :]
                slab_out[rows_re, :] = nr
                slab_out[rows_im, :] = ni
                h[2 * dd], h[2 * dd + 1] = nr, ni
        hs = jnp.concatenate([slab_in[pl.ds(section(d, 0) + gp * pitch, tc), :],
                              slab_in[pl.ds(section(d, 1) + gp * pitch, tc), :]], axis=1).astype(BF16)
        part = jnp.dot(hs, c_ref[d, gp], preferred_element_type=F32)
        y_acc[d, ch] = part if (d, ch) not in y_acc else y_acc[d, ch] + part
        if gp % GP_PER_CHUNK == GP_PER_CHUNK - 1:
            y_ref[:, ch * LANES:(ch + 1) * LANES] = y_acc[d, ch]
    return h


def _s5_kernel(uf_ref, ub_ref, h0_ref, lam_ref, b_ref, c_ref, yf_ref, yb_ref, fin_ref,
               drive0, drive1, slab0, slab1, carry):
    s = pl.program_id(0)

    @pl.when(s == 0)
    def _():
        for buf in (drive0, drive1, slab0, slab1, carry):
            buf[...] = jnp.zeros_like(buf)

    item = s - 1
    valid = (item >= 0) & (item < N_SCAN_ITEMS)
    first = valid & ((item < BATCH) | ((item - BATCH) % S_CHUNKS == 0))
    h = [jnp.where(first, h0_ref[0, k], carry[k]) for k in range(4)]

    def run(drive_in, drive_out, slab_in, slab_out):
        new = _s5_stages(uf_ref, ub_ref, lam_ref, b_ref, c_ref, yf_ref, yb_ref, h,
                         drive_in, drive_out, slab_in, slab_out)
        for k in range(4):
            kept = jnp.where(valid, new[k], h[k])
            carry[k] = kept
            fin_ref[0, k] = kept

    @pl.when(s % 2 == 0)
    def _():
        run(drive1, drive0, slab0, slab1)

    @pl.when(s % 2 == 1)
    def _():
        run(drive0, drive1, slab1, slab0)


def _s5_scan(layer, u, h0, lam_bar, b_pad, c_pad):
    tc = SCAN_CHUNK
    n_seq = BATCH + DEC_BATCH
    last = N_SCAN_ITEMS - 1
    drive_item = lambda s: jnp.minimum(s, last)
    scan_item = lambda s: jnp.clip(s - 1, 0, last)
    read_item = lambda s: jnp.clip(s - 2, 0, last)
    rows = lambda item, which: pl.BlockSpec((tc, SSM_WIDTH), lambda s: (_scan_item(item(s))[which], 0))
    state = pl.BlockSpec((1, 4, N_GP, LANES), lambda s: (_scan_item(scan_item(s))[0], 0, 0, 0))
    state_in = pl.BlockSpec((1, None, 4, N_GP, LANES), lambda s: (_scan_item(scan_item(s))[0], layer, 0, 0, 0))
    slab = pltpu.VMEM((4 * N_GP * SCAN_PITCH, LANES), F32)
    return pl.pallas_call(
        _s5_kernel,
        grid=(N_SCAN_ITEMS + 2,),
        in_specs=[rows(drive_item, 1), rows(drive_item, 2), state_in, _layer_spec(lam_bar, layer),
                  _layer_spec(b_pad, layer), _layer_spec(c_pad, layer)],
        out_specs=[rows(read_item, 1), rows(read_item, 2), state],
        out_shape=[jax.ShapeDtypeStruct((ROWS, SSM_WIDTH), F32),
                   jax.ShapeDtypeStruct((ROWS, SSM_WIDTH), F32),
                   jax.ShapeDtypeStruct((n_seq, 4, N_GP, LANES), F32)],
        scratch_shapes=[slab, slab, slab, slab, pltpu.VMEM((4, N_GP, LANES), F32)],
        compiler_params=_params(1, 56),
        name="s5_scan",
    )(u, u, h0, lam_bar, b_pad, c_pad)


def _s5_discretise(lam_re, lam_im, log_step, b_re, b_im, c_re, c_im):
    delta = jnp.exp(log_step)[..., None]
    mag = jnp.exp(lam_re * delta)
    lbr = mag * jnp.cos(lam_im * delta)
    lbi = mag * jnp.sin(lam_im * delta)
    den = lam_re * lam_re + lam_im * lam_im
    cr = ((lbr - 1.0) * lam_re + lbi * lam_im) / den
    ci = (lbi * lam_re - (lbr - 1.0) * lam_im) / den
    bbr = cr[..., None] * b_re - ci[..., None] * b_im
    bbi = cr[..., None] * b_im + ci[..., None] * b_re
    lam_bar = jnp.stack([lbr, lbi], axis=2).reshape(DEPTH, 2, 2, N_GP, LANES)

    def pair_split(a):
        a = a.reshape(DEPTH, 2, N_GP, 2, *a.shape[3:])
        return a[:, :, :, 0], a[:, :, :, 1]

    slot = jnp.arange(N_GP) % GP_PER_CHUNK

    def place(a, axis):
        parts = [jnp.where((slot == s)[None, None, :, None, None], a, 0.0) for s in range(GP_PER_CHUNK)]
        return jnp.concatenate(parts, axis=axis)

    cat = jnp.concatenate
    re_e, re_o = pair_split(jnp.swapaxes(bbr, -1, -2))
    im_e, im_o = pair_split(jnp.swapaxes(bbi, -1, -2))
    z = jnp.zeros_like(re_e)
    b_small = cat([cat([re_e, z, im_e, z], -1), cat([z, re_o, z, im_o], -1)], -2)
    b_pad = place(b_small, -2)
    cre_e, cre_o = pair_split(jnp.swapaxes(c_re, -1, -2))
    cim_e, cim_o = pair_split(-jnp.swapaxes(c_im, -1, -2))
    z = jnp.zeros_like(cre_e)
    c_small = cat([cat([cre_e, z], -1), cat([z, cre_o], -1), cat([cim_e, z], -1), cat([z, cim_o], -1)], -2)
    c_pad = place(c_small, -1)
    return lam_bar, b_pad.astype(BF16), c_pad.astype(BF16)


def _outproj_kernel(*refs, moe):
    (attn_p_ref, attn_s_ref, yf_ref, yb_ref, u_ref, xp_ref, xs_ref, g1_ref, sh2_ref, sc2_ref, d_ref, wglu_ref,
     bglu_ref, wout_ref, lng_ref, lnb_ref) = refs[:16]
    if moe:
        router_ref, x1_ref, h2_ref, meta_ref, meta_t_ref, cnt_ref, wglu_b, wout_b, cnt = refs[16:]
    else:
        x1_ref, h2_ref, wglu_b, wout_b = refs[16:]
    attn = _group_rows(attn_p_ref, attn_s_ref)
    x = _group_rows(xp_ref, xs_ref)

    @pl.when(pl.program_id(0) == 0)
    def _():
        wglu_b[...] = wglu_ref[...].astype(BF16)
        wout_b[...] = wout_ref[...].astype(BF16)

    y = yf_ref[...] + yb_ref[...] + d_ref[...] * u_ref[...]
    g = 0.5 * y * (1.0 + jnp.tanh(math.sqrt(2.0 / math.pi) * (y + 0.044715 * (y * y * y))))
    z = jnp.dot(g.astype(BF16), wglu_b[...], preferred_element_type=F32) + bglu_ref[...]
    y_ssm = g * _sigmoid(z)
    mix = (jnp.dot(attn.astype(BF16), wout_b[:ATTN_WIDTH, :], preferred_element_type=F32)
           + jnp.dot(y_ssm.astype(BF16), wout_b[ATTN_WIDTH:, :], preferred_element_type=F32))
    x1 = _layer_norm(DEEPNORM_ALPHA * x + g1_ref[0] * mix) * lng_ref[...] + lnb_ref[...]
    x1_ref[...] = x1
    h2 = _layer_norm(x1) * (1.0 + sc2_ref[0]) + sh2_ref[0]
    h2_ref[...] = h2

    if moe:
        @pl.when(pl.program_id(0) == 0)
        def _():
            cnt[...] = jnp.zeros_like(cnt)

        router = router_ref[...]
        h_hi, r_hi = h2.astype(BF16), router.astype(BF16)
        h_lo = (h2 - h_hi.astype(F32)).astype(BF16)
        r_lo = (router - r_hi.astype(F32)).astype(BF16)
        logits = (jnp.dot(h_hi, r_hi, preferred_element_type=F32)
                  + (jnp.dot(h_hi, r_lo, preferred_element_type=F32)
                     + jnp.dot(h_lo, r_hi, preferred_element_type=F32)))
        tm = logits.shape[0]
        lane = lax.broadcasted_iota(jnp.int32, logits.shape, 1).astype(F32)
        neg = jnp.float32(-jnp.inf)
        l1 = jnp.where(lane < N_EXPERTS, logits, neg)
        m1 = jnp.max(l1, axis=-1, keepdims=True)
        i1 = jnp.min(jnp.where(l1 == m1, lane, float(LANES)), axis=-1, keepdims=True)
        l2 = jnp.where(lane == i1, neg, l1)
        m2 = jnp.max(l2, axis=-1, keepdims=True)
        i2 = jnp.min(jnp.where(l2 == m2, lane, float(LANES)), axis=-1, keepdims=True)
        e2 = jnp.exp(m2 - m1)
        w1 = 1.0 / (1.0 + e2)
        w2 = e2 / (1.0 + e2)
        hit = jnp.where((lane == i1) | (lane == i2), 1.0, 0.0)
        r = lax.broadcasted_iota(jnp.int32, (tm, tm), 0)
        c = lax.broadcasted_iota(jnp.int32, (tm, tm), 1)
        earlier = jnp.where(c < r, 1.0, 0.0).astype(BF16)
        rank = cnt[...] + jnp.dot(earlier, hit.astype(BF16), preferred_element_type=F32)
        r1 = jnp.sum(jnp.where(lane == i1, rank, 0.0), axis=-1, keepdims=True)
        r2 = jnp.sum(jnp.where(lane == i2, rank, 0.0), axis=-1, keepdims=True)
        cnt[...] = cnt[...] + jnp.sum(hit, axis=0, keepdims=True)
        cnt_ref[...] = cnt[...]
        fields = (i1, i2, r1, r2, w1, w2)
        meta = jnp.zeros_like(logits)
        for k, f in enumerate(fields):
            meta = jnp.where(lane == float(k), f, meta)
        meta_ref[...] = meta
        meta_t_ref[...] = meta.T[:meta_t_ref.shape[0], :]


def _out_projection(layer, attn_p, attn_s, yf, yb, u, x_pair, mod3, d_skip, w_glu, b_glu, w_out, ln_g, ln_b,
                    router):
    tm = TM_PROJ
    moe = router is not None
    row = lambda w: pl.BlockSpec((tm, w), lambda i: (i, 0))
    full = lambda a: pl.BlockSpec(a.shape, lambda i: (0,) * a.ndim)
    args = [attn_p, attn_s, yf, yb, u, *x_pair, mod3, mod3, mod3, d_skip, w_glu, b_glu, w_out, ln_g, ln_b]
    in_specs = (_group_specs(tm, ATTN_WIDTH, False)
                + [row(SSM_WIDTH), row(SSM_WIDTH), row(SSM_WIDTH)]
                + _group_specs(tm, D_MODEL, x_pair[0] is x_pair[1])
                + [_mod_spec(layer, 2, tm), _mod_spec(layer, 3, tm), _mod_spec(layer, 4, tm),
                   full(d_skip), _layer_spec(w_glu, layer), full(b_glu), _layer_spec(w_out, layer),
                   full(ln_g), full(ln_b)])
    out_specs = [row(D_MODEL), row(D_MODEL)]
    out_shape = [jax.ShapeDtypeStruct((ROWS, D_MODEL), F32), jax.ShapeDtypeStruct((ROWS, D_MODEL), F32)]
    scratch = [pltpu.VMEM((SSM_WIDTH, SSM_WIDTH), BF16), pltpu.VMEM((D_MODEL, D_MODEL), BF16)]
    if moe:
        args.append(router)
        in_specs.append(_layer_spec(router, layer // 2))
        out_specs += [row(LANES), pl.BlockSpec((META_ROWS, tm), lambda i: (0, i)),
                      pl.BlockSpec((1, LANES), lambda i: (0, 0))]
        out_shape += [jax.ShapeDtypeStruct((ROWS, LANES), F32), jax.ShapeDtypeStruct((META_ROWS, ROWS), F32),
                      jax.ShapeDtypeStruct((1, LANES), F32)]
        scratch.append(pltpu.VMEM((1, LANES), F32))
    return pl.pallas_call(
        functools.partial(_outproj_kernel, moe=moe),
        grid=(ROWS // tm,),
        in_specs=in_specs,
        out_specs=out_specs,
        out_shape=out_shape,
        scratch_shapes=scratch,
        compiler_params=_params(1, 48),
        name="out_projection",
    )(*args)


def _swiglu_partial(xb, wg, wu, wd):
    a = jnp.dot(xb, wg, preferred_element_type=F32)
    b = jnp.dot(xb, wu, preferred_element_type=F32)
    act = a * _sigmoid(a) * b
    return act, lambda act_: jnp.dot(act_.astype(BF16), wd, preferred_element_type=F32)


def _ffn_kernel(h_ref, x1_ref, g2_ref, wg_ref, wu_ref, wd_ref, lng_ref, lnb_ref, o_ref, hb, acc, *, tf, d_ff):
    j = pl.program_id(1)

    @pl.when(j == 0)
    def _():
        hb[...] = h_ref[...].astype(BF16)
        acc[...] = jnp.zeros_like(acc)

    last = pl.num_programs(1) - 1
    tail = d_ff % tf

    def block(width, valid):
        wd = wd_ref[:width, :]
        if valid < width:
            r = lax.broadcasted_iota(jnp.int32, wd.shape, 0)
            wd = jnp.where(r < valid, wd, 0.0)
        act, down = _swiglu_partial(hb[...], wg_ref[:, :width].astype(BF16), wu_ref[:, :width].astype(BF16),
                                    wd.astype(BF16))
        if valid < width:
            col = lax.broadcasted_iota(jnp.int32, act.shape, 1)
            act = jnp.where(col < valid, act, 0.0)
        acc[...] += down(act)

    if tail:
        @pl.when(j < last)
        def _():
            block(tf, tf)

        @pl.when(j == last)
        def _():
            block(-(-tail // LANES) * LANES, tail)
    else:
        block(tf, tf)

    @pl.when(j == last)
    def _():
        y = DEEPNORM_ALPHA * x1_ref[...] + g2_ref[0] * acc[...]
        o_ref[...] = _layer_norm(y) * lng_ref[...] + lnb_ref[...]


def _ffn(layer, h2, x1, mod3, wg, wu, wd, ln_g, ln_b):
    tm, tf, d_ff = TM_FFN, TF_DENSE, D_FF
    w_idx = layer // 2
    up = pl.BlockSpec((None, D_MODEL, tf), lambda i, j: (w_idx, 0, j))
    down = pl.BlockSpec((None, tf, D_MODEL), lambda i, j: (w_idx, j, 0))
    row = lambda w: pl.BlockSpec((tm, w), lambda i, j: (i, 0))
    vec = pl.BlockSpec((1, D_MODEL), lambda i, j: (0, 0))
    g2 = pl.BlockSpec((1, 1, D_MODEL), lambda i, j: (layer * N_MOD + _mod_row(i, tm), 0, 5))
    return pl.pallas_call(
        functools.partial(_ffn_kernel, tf=tf, d_ff=d_ff),
        grid=(ROWS // tm, pl.cdiv(d_ff, tf)),
        in_specs=[row(D_MODEL), row(D_MODEL), g2, up, up, down, vec, vec],
        out_specs=row(D_MODEL),
        out_shape=jax.ShapeDtypeStruct((ROWS, D_MODEL), F32),
        scratch_shapes=[pltpu.VMEM((tm, D_MODEL), BF16), pltpu.VMEM((tm, D_MODEL), F32)],
        compiler_params=_params(2, 48),
        name="ffn_dense",
    )(h2, x1, mod3, wg, wu, wd, ln_g, ln_b)


def _route_plan(meta_t, counts):
    i32 = jnp.int32
    e1, e2 = meta_t[0].astype(i32), meta_t[1].astype(i32)
    r1, r2 = meta_t[2].astype(i32), meta_t[3].astype(i32)
    cnt = counts[0, :N_EXPERTS].astype(i32)
    tiles = (cnt + MOE_TILE - 1) // MOE_TILE
    tile_end = jnp.cumsum(tiles)
    tile_start = tile_end - tiles
    base = tile_start * MOE_TILE
    pos1 = (base[e1] + r1) * ROW_TILE
    pos2 = (base[e2] + r2) * ROW_TILE
    t = jnp.arange(MOE_TILES, dtype=i32)
    n_used = tile_end[-1]
    used = t < n_used
    owner = jnp.sum((t[:, None] >= tile_end[None, :]).astype(i32), axis=1)
    owner = jnp.minimum(owner, N_EXPERTS - 1)
    rows = jnp.clip(cnt[owner] - (t - tile_start[owner]) * MOE_TILE, 0, MOE_TILE)
    rows = jnp.where(used, rows, 0)
    n_sub = (rows + MOE_SUB - 1) // MOE_SUB
    last_owner = owner[jnp.maximum(n_used - 1, 0)]
    tile_expert = jnp.where(used, owner, last_owner)
    zero_fill = (rows < MOE_TILE).astype(i32)
    return pos1, pos2, tile_expert, n_sub, zero_fill


def _slot_rows(start):
    return pl.ds(pl.multiple_of(start, ROW_TILE), ROW_TILE)


def _dispatch_kernel(zf_ref, p1_ref, p2_ref, h_ref, xs_ref, src0, src1, zbuf, zsem, sem0, sem1):
    tm = h_ref.shape[0]
    tile_rows = MOE_TILE * ROW_TILE

    def zero_copy(k):
        return pltpu.make_async_copy(zbuf, xs_ref.at[pl.ds(k * tile_rows, tile_rows)], zsem)

    @pl.when(pl.program_id(0) == 0)
    def _():
        zbuf[...] = jnp.zeros_like(zbuf)
        for k in range(MOE_TILES):
            @pl.when(zf_ref[k] != 0)
            def _():
                zero_copy(k).start()
        for k in range(MOE_TILES):
            @pl.when(zf_ref[k] != 0)
            def _():
                zero_copy(k).wait()

    step = pl.program_id(0)

    def row_copy(src, sem, r, slot):
        return pltpu.make_async_copy(src.at[_slot_rows(r * ROW_TILE)], xs_ref.at[_slot_rows(slot)], sem)

    def drain(src, sem):
        def wait(r, carry):
            row_copy(src, sem, 0, 0).wait()
            row_copy(src, sem, 0, 0).wait()
            return carry
        lax.fori_loop(0, tm, wait, 0, unroll=DMA_UNROLL)

    def scatter(src, sem, src_prev, sem_prev):
        for s in range(ROW_TILE):
            src[pl.ds(s, tm, stride=ROW_TILE), :] = h_ref[:, s * LANES:(s + 1) * LANES]

        def start(r, carry):
            row_copy(src, sem, r, p1_ref[r]).start()
            row_copy(src, sem, r, p2_ref[r]).start()
            return carry

        lax.fori_loop(0, tm, start, 0, unroll=DMA_UNROLL)

        @pl.when(step > 0)
        def _():
            drain(src_prev, sem_prev)

        @pl.when(step == pl.num_programs(0) - 1)
        def _():
            drain(src, sem)

    @pl.when(step % 2 == 0)
    def _():
        scatter(src0, sem0, src1, sem1)

    @pl.when(step % 2 == 1)
    def _():
        scatter(src1, sem1, src0, sem0)


def _moe_dispatch(h2, pos1, pos2, zero_fill):
    tm = TM_ROUTE
    pos = pl.BlockSpec((tm,), lambda i, zf: (i,), memory_space=pltpu.SMEM)
    return pl.pallas_call(
        _dispatch_kernel,
        grid_spec=pltpu.PrefetchScalarGridSpec(
            num_scalar_prefetch=1,
            grid=(ROWS // tm,),
            in_specs=[pos, pos, pl.BlockSpec((tm, D_MODEL), lambda i, zf: (i, 0))],
            out_specs=pl.BlockSpec(memory_space=pl.ANY),
            scratch_shapes=[pltpu.VMEM((tm * ROW_TILE, LANES), F32), pltpu.VMEM((tm * ROW_TILE, LANES), F32),
                            pltpu.VMEM((MOE_TILE * ROW_TILE, LANES), F32),
                            pltpu.SemaphoreType.DMA(()), pltpu.SemaphoreType.DMA(()),
                            pltpu.SemaphoreType.DMA(())]),
        out_shape=jax.ShapeDtypeStruct((MOE_ROWS * ROW_TILE, LANES), F32),
        compiler_params=_params(1, 32),
        name="moe_dispatch",
    )(zero_fill, pos1, pos2, h2)


def _expert_kernel(te_ref, ns_ref, xs_ref, wg_ref, wu_ref, wd_ref, ys_ref, xb, acc):
    i, j = pl.program_id(0), pl.program_id(1)
    n_sub = ns_ref[i]

    @pl.when(j == 0)
    def _():
        for s in range(ROW_TILE):
            xb[:, s * LANES:(s + 1) * LANES] = xs_ref[pl.ds(s, MOE_TILE, stride=ROW_TILE), :].astype(BF16)
        acc[...] = jnp.zeros_like(acc)

    def accumulate(rows):
        act, down = _swiglu_partial(xb[rows, :], wg_ref[...].astype(BF16), wu_ref[...].astype(BF16),
                                    wd_ref[...].astype(BF16))
        acc[rows, :] += down(act)

    full = MOE_TILE // MOE_SUB

    @pl.when(n_sub == full)
    def _():
        accumulate(slice(None))

    @pl.when((n_sub > 0) & (n_sub < full))
    def _():
        def body(s, carry):
            accumulate(pl.ds(pl.multiple_of(s * MOE_SUB, MOE_SUB), MOE_SUB))
            return carry
        lax.fori_loop(0, n_sub, body, 0)

    @pl.when(j == pl.num_programs(1) - 1)
    def _():
        for s in range(ROW_TILE):
            ys_ref[pl.ds(s, MOE_TILE, stride=ROW_TILE), :] = acc[:, s * LANES:(s + 1) * LANES]


def _moe_experts(layer, xs, tile_expert, n_sub, wg, wu, wd):
    tf = TF_MOE
    n_j = EXPERT_FF // tf
    w_idx = layer // 2
    ff = lambda i, j, ns: jnp.where(ns[i] > 0, j, n_j - 1)
    up = pl.BlockSpec((None, None, D_MODEL, tf), lambda i, j, te, ns: (w_idx, te[i], 0, ff(i, j, ns)))
    down = pl.BlockSpec((None, None, tf, D_MODEL), lambda i, j, te, ns: (w_idx, te[i], ff(i, j, ns), 0))
    rows = pl.BlockSpec((MOE_TILE * ROW_TILE, LANES), lambda i, j, te, ns: (i, 0))
    return pl.pallas_call(
        _expert_kernel,
        grid_spec=pltpu.PrefetchScalarGridSpec(
            num_scalar_prefetch=2,
            grid=(MOE_TILES, n_j),
            in_specs=[rows, up, up, down],
            out_specs=rows,
            scratch_shapes=[pltpu.VMEM((MOE_TILE, D_MODEL), BF16), pltpu.VMEM((MOE_TILE, D_MODEL), F32)]),
        out_shape=jax.ShapeDtypeStruct((MOE_ROWS * ROW_TILE, LANES), F32),
        compiler_params=_params(2, 48),
        name="moe_experts",
    )(tile_expert, n_sub, xs, wg, wu, wd)


def _combine_kernel(p1_ref, p2_ref, ys_ref, x1_ref, meta_ref, g2_ref, lng_ref, lnb_ref, *rest, split):
    b1, b2, f, sem = rest[-4:]
    tm = x1_ref.shape[0]

    def row_copy(r, p_ref, buf):
        return pltpu.make_async_copy(ys_ref.at[_slot_rows(p_ref[r])], buf.at[_slot_rows(r * ROW_TILE)], sem)

    def start(r, carry):
        row_copy(r, p1_ref, b1).start()
        row_copy(r, p2_ref, b2).start()
        return carry

    def wait(r, carry):
        row_copy(r, p1_ref, b1).wait()
        row_copy(r, p2_ref, b2).wait()
        return carry

    lax.fori_loop(0, tm, start, 0, unroll=DMA_UNROLL)
    lax.fori_loop(0, tm, wait, 0, unroll=DMA_UNROLL)
    meta = meta_ref[...]
    w1, w2 = meta[:, 4:5], meta[:, 5:6]
    for s in range(ROW_TILE):
        rows = pl.ds(s, tm, stride=ROW_TILE)
        f[:, s * LANES:(s + 1) * LANES] = w1 * b1[rows, :] + w2 * b2[rows, :]
    y = DEEPNORM_ALPHA * x1_ref[...] + g2_ref[0] * f[...]
    out = _layer_norm(y) * lng_ref[...] + lnb_ref[...]
    if split:
        op_ref, os_ref = rest[:2]
        is_prompt = pl.program_id(0) < P_ROWS // tm

        @pl.when(is_prompt)
        def _():
            op_ref[...] = out

        @pl.when(jnp.logical_not(is_prompt))
        def _():
            os_ref[...] = out
    else:
        rest[0][...] = out


def _moe_combine(layer, ys, pos1, pos2, x1, meta, mod3, ln_g, ln_b, split):
    tm = TM_COMBINE
    pos = pl.BlockSpec((tm,), lambda i: (i,), memory_space=pltpu.SMEM)
    row = lambda w: pl.BlockSpec((tm, w), lambda i: (i, 0))
    vec = pl.BlockSpec((1, D_MODEL), lambda i: (0, 0))
    if split:
        out_specs = _group_specs(tm, D_MODEL, False)
        out_shape = [jax.ShapeDtypeStruct((P_ROWS, D_MODEL), F32), jax.ShapeDtypeStruct((S_ROWS, D_MODEL), F32)]
    else:
        out_specs = row(D_MODEL)
        out_shape = jax.ShapeDtypeStruct((ROWS, D_MODEL), F32)
    return pl.pallas_call(
        functools.partial(_combine_kernel, split=split),
        grid=(ROWS // tm,),
        in_specs=[pos, pos, pl.BlockSpec(memory_space=pl.ANY), row(D_MODEL), row(LANES),
                  _mod_spec(layer, 5, tm), vec, vec],
        out_specs=out_specs,
        out_shape=out_shape,
        scratch_shapes=[pltpu.VMEM((tm * ROW_TILE, LANES), F32), pltpu.VMEM((tm * ROW_TILE, LANES), F32),
                        pltpu.VMEM((tm, D_MODEL), F32), pltpu.SemaphoreType.DMA(())],
        compiler_params=_params(1, 32),
        name="moe_combine",
    )(pos1, pos2, ys, x1, meta, mod3, ln_g, ln_b)


def _rope_tables():
    rows = DEC_SEQ // GRID_W
    row = jnp.repeat(jnp.arange(rows), GRID_W).astype(F32)
    col = jnp.tile(jnp.arange(GRID_W), rows).astype(F32)
    freqs = ROPE_THETA ** (-jnp.arange(ROPE_PAIRS, dtype=F32) / ROPE_PAIRS)
    ang = jnp.stack([row[:, None] * freqs, col[:, None] * freqs], axis=1)
    cos, sin = jnp.cos(ang), jnp.sin(ang)
    cos_h = jnp.concatenate([cos[:, 0], cos[:, 0], cos[:, 1], cos[:, 1]], axis=-1)
    sin_h = jnp.concatenate([-sin[:, 0], sin[:, 0], -sin[:, 1], sin[:, 1]], axis=-1)
    per_tile = LANES // HEAD_DIM
    cos_t = jnp.concatenate([jnp.ones((ROPE_ID_ROWS, LANES), F32), jnp.tile(cos_h, (1, per_tile))], axis=0)
    sin_t = jnp.concatenate([jnp.zeros((ROPE_ID_ROWS, LANES), F32), jnp.tile(sin_h, (1, per_tile))], axis=0)
    return cos_t, sin_t


def kernel(x_prompt, x_sample, c, cache_k, cache_v, state_ssm_re, state_ssm_im, c_ctx, w_ada, b_ada,
           w_in, q_gain, k_gain, ssm_lambda_re, ssm_lambda_im, ssm_log_step, ssm_b_re, ssm_b_im,
           ssm_c_re, ssm_c_im, ssm_d, w_glu, b_glu, w_out, ln1_g, ln1_b, ln2_g, ln2_b,
           ffn_w_gate, ffn_w_up, ffn_w_down, router_w, moe_w_gate, moe_w_up, moe_w_down):
    x_pair = (x_prompt.reshape(P_ROWS, D_MODEL), x_sample.reshape(S_ROWS, D_MODEL))
    cvec = jnp.concatenate([c_ctx[None], c, jnp.zeros((N_MOD - 1 - DEC_BATCH, D_MODEL), F32)], axis=0)
    mod3 = _modulation(cvec, w_ada, b_ada).reshape(DEPTH * N_MOD, 1, 6 * D_MODEL)
    rope_cos, rope_sin = _rope_tables()
    lam_bar, b_pad, c_pad = _s5_discretise(ssm_lambda_re, ssm_lambda_im, ssm_log_step,
                                           ssm_b_re, ssm_b_im, ssm_c_re, ssm_c_im)
    cache_k4 = cache_k.reshape(DEC_BATCH, DEPTH, PAST_LEN, KV_WIDTH)
    cache_v4 = cache_v.reshape(DEC_BATCH, DEPTH, PAST_LEN, KV_WIDTH)
    h0_s = jnp.stack([state_ssm_re, state_ssm_im], axis=3)
    h0_s = h0_s.reshape(DEC_BATCH, DEPTH, 4, N_GP, LANES)
    h0_all = jnp.concatenate([jnp.zeros((BATCH, DEPTH, 4, N_GP, LANES), F32), h0_s], axis=0)
    router_pad = jnp.pad(router_w, ((0, 0), (0, 0), (0, LANES - N_EXPERTS)))
    gain2 = lambda g: jnp.tile(g, (1, LANES // HEAD_DIM))

    new_k, new_v, new_re, new_im = [], [], [], []
    for l in range(DEPTH):
        q, k_rot, k_norm, v, u = _in_projection(l, x_pair, mod3, w_in, gain2(q_gain[l:l + 1]),
                                                gain2(k_gain[l:l + 1]), rope_cos, rope_sin)
        attn_p, attn_s = _attention(l, q, k_rot, v, cache_k4, cache_v4)
        yf, yb, fin = _s5_scan(l, u, h0_all, lam_bar, b_pad, c_pad)
        moe = l % 2 == 1
        outs = _out_projection(l, attn_p, attn_s, yf, yb, u, x_pair, mod3, ssm_d[l:l + 1], w_glu,
                               b_glu[l:l + 1], w_out, ln1_g[l:l + 1], ln1_b[l:l + 1],
                               router_pad if moe else None)
        if moe:
            x1, h2, meta, meta_t, counts = outs
            pos1, pos2, tile_expert, n_sub, zero_fill = _route_plan(meta_t, counts)
            xs = _moe_dispatch(h2, pos1, pos2, zero_fill)
            ys = _moe_experts(l, xs, tile_expert, n_sub, moe_w_gate, moe_w_up, moe_w_down)
            last = l == DEPTH - 1
            x = _moe_combine(l, ys, pos1, pos2, x1, meta, mod3, ln2_g[l:l + 1], ln2_b[l:l + 1], split=last)
            x_pair = tuple(x) if last else (x, x)
        else:
            x1, h2 = outs
            x = _ffn(l, h2, x1, mod3, ffn_w_gate, ffn_w_up, ffn_w_down, ln2_g[l:l + 1], ln2_b[l:l + 1])
            x_pair = (x, x)
        new_k.append(k_norm[:P_ROWS].reshape(BATCH, SEQ, N_KV_HEADS, HEAD_DIM))
        new_v.append(v[:P_ROWS].reshape(BATCH, SEQ, N_KV_HEADS, HEAD_DIM))
        fin_p = fin[:BATCH].reshape(BATCH, 2, 2, SSM_GROUPS, SSM_STATE)
        new_re.append(fin_p[:, :, 0])
        new_im.append(fin_p[:, :, 1])

    assert DEPTH % 2 == 0, "the last layer is the routed one and returns the two groups separately"
    y_prompt = x_pair[0].reshape(BATCH, SEQ, D_MODEL)
    y_sample = x_pair[1].reshape(DEC_BATCH, DEC_SEQ, D_MODEL)
    return (y_prompt, y_sample, jnp.stack(new_k, axis=1), jnp.stack(new_v, axis=1),
            jnp.stack(new_re, axis=1), jnp.stack(new_im, axis=1))
```

```python
import functools
import math

import jax
import jax.numpy as jnp
from jax import lax
from jax.experimental import pallas as pl
from jax.experimental.pallas import tpu as pltpu

D_MODEL = 1024
BATCH = 16
SEQ = 256
DEPTH = 4
DEC_BATCH = 2
DEC_SEQ = 2048
PAST_LEN = 256
GRID_W = 64
N_HEADS = 8
N_KV_HEADS = 2
HEAD_DIM = 64
ATTN_WIDTH = N_HEADS * HEAD_DIM
KV_WIDTH = N_KV_HEADS * HEAD_DIM
SSM_WIDTH = D_MODEL - ATTN_WIDTH
SSM_GROUP_CH = 16
SSM_GROUPS = SSM_WIDTH // SSM_GROUP_CH
SSM_STATE = 64
IN_WIDTH = ATTN_WIDTH + 2 * KV_WIDTH + SSM_WIDTH
ROPE_THETA = 10000.0
ROPE_PAIRS = HEAD_DIM // 4
D_FF = 2752
N_EXPERTS = 8
TOP_K = 2
EXPERT_FF = 3584
DEEPNORM_ALPHA = (2 * DEPTH) ** 0.25
NORM_EPS = 1e-6

F32 = jnp.float32
BF16 = jnp.bfloat16
HIGHEST = lax.Precision.HIGHEST
LANES = 128

P_ROWS = BATCH * SEQ
S_ROWS = DEC_BATCH * DEC_SEQ
ROWS = P_ROWS + S_ROWS
N_MOD = 8
GRP = N_HEADS // N_KV_HEADS
SCALE = HEAD_DIM ** -0.5

TM_PROJ = 512
ROPE_ID_ROWS = TM_PROJ
TQ_PROMPT = SEQ
TQ_SAMPLE = 128
SCAN_CHUNK = 256
SCAN_PITCH = SCAN_CHUNK + 4
N_GP = SSM_GROUPS // 2
GP_PER_CHUNK = LANES // (2 * SSM_GROUP_CH)
S_CHUNKS = DEC_SEQ // SCAN_CHUNK
N_SCAN_ITEMS = BATCH * (SEQ // SCAN_CHUNK) + DEC_BATCH * S_CHUNKS
TM_FFN = 1024
TF_DENSE = 512
TF_MOE = 512
MOE_TILE = 1024
MOE_SUB = 256
MOE_TILES = TOP_K * ROWS // MOE_TILE + N_EXPERTS
MOE_ROWS = MOE_TILES * MOE_TILE
TM_ROUTE = 1024
TM_COMBINE = 512
META_ROWS = 8
ROW_TILE = 8
DMA_UNROLL = 8
MIB = 1024 * 1024


def _params(n_axes, vmem_mib):
    return pltpu.CompilerParams(dimension_semantics=("arbitrary",) * n_axes,
                                vmem_limit_bytes=vmem_mib * MIB)


def _layer_norm(x):
    mu = jnp.mean(x, axis=-1, keepdims=True)
    xc = x - mu
    var = jnp.mean(xc * xc, axis=-1, keepdims=True)
    return xc * lax.rsqrt(var + NORM_EPS)


def _sigmoid(x):
    return 1.0 / (1.0 + jnp.exp(-x))


def _mod_row(tile, tm):
    p_tiles = P_ROWS // tm
    return jnp.where(tile < p_tiles, 0, 1 + (tile - p_tiles) // (DEC_SEQ // tm))


def _mod_kernel(c_ref, w_ref, b_ref, o_ref):
    c = c_ref[...]
    a = c * _sigmoid(c)
    o_ref[0] = jnp.dot(a, w_ref[0], precision=HIGHEST, preferred_element_type=F32) + b_ref[0]


def _modulation(cvec, w_ada, b_ada):
    tn = 3072
    return pl.pallas_call(
        _mod_kernel,
        grid=(DEPTH, 6 * D_MODEL // tn),
        in_specs=[pl.BlockSpec((N_MOD, D_MODEL), lambda l, j: (0, 0)),
                  pl.BlockSpec((1, D_MODEL, tn), lambda l, j: (l, 0, j)),
                  pl.BlockSpec((1, 1, tn), lambda l, j: (l, 0, j))],
        out_specs=pl.BlockSpec((1, N_MOD, tn), lambda l, j: (l, 0, j)),
        out_shape=jax.ShapeDtypeStruct((DEPTH, N_MOD, 6 * D_MODEL), F32),
        compiler_params=_params(2, 40),
        name="adaln_mod",
    )(cvec, w_ada, b_ada.reshape(DEPTH, 1, 6 * D_MODEL))


def _layer_spec(stacked, index):
    zeros = (0,) * (stacked.ndim - 1)
    return pl.BlockSpec((None,) + stacked.shape[1:], lambda i: (index,) + zeros)


def _mod_spec(layer, part, tm):
    return pl.BlockSpec((1, 1, D_MODEL), lambda i: (layer * N_MOD + _mod_row(i, tm), 0, part))


def _group_rows(p_ref, s_ref):
    is_prompt = pl.program_id(0) < P_ROWS // p_ref.shape[0]
    return jnp.where(is_prompt, p_ref[...], s_ref[...])


def _group_specs(tm, width, stacked):
    p_tiles = P_ROWS // tm
    offset = p_tiles if stacked else 0
    return [pl.BlockSpec((tm, width), lambda i: (jnp.minimum(i, p_tiles - 1), 0)),
            pl.BlockSpec((tm, width), lambda i: (jnp.maximum(i - p_tiles, 0) + offset, 0))]


def _inproj_kernel(xp_ref, xs_ref, sh_ref, sc_ref, w_ref, qg_ref, kg_ref, cos_ref, sin_ref,
                   q_ref, kr_ref, kn_ref, v_ref, u_ref, wb_ref):
    @pl.when(pl.program_id(0) == 0)
    def _():
        wb_ref[...] = w_ref[...].astype(BF16)

    h = _layer_norm(_group_rows(xp_ref, xs_ref)) * (1.0 + sc_ref[0]) + sh_ref[0]
    proj = jnp.dot(h.astype(BF16), wb_ref[...], preferred_element_type=F32)

    tm = proj.shape[0]
    head_of = lambda idx: lax.shift_right_logical(idx, int(math.log2(HEAD_DIM)))
    r = head_of(lax.broadcasted_iota(jnp.int32, (LANES, LANES), 0))
    c = head_of(lax.broadcasted_iota(jnp.int32, (LANES, LANES), 1))
    seg = jnp.where(r == c, 1.0 / HEAD_DIM, 0.0).astype(BF16)
    lane = lax.broadcasted_iota(jnp.int32, (tm, LANES), 1)
    first_half = (lane & (2 * ROPE_PAIRS - 1)) < ROPE_PAIRS
    cos = cos_ref[...]
    sin = sin_ref[...]

    def head_norm(t, gain):
        sq = t * t
        hi = sq.astype(BF16)
        lo = (sq - hi.astype(F32)).astype(BF16)
        ms = (jnp.dot(hi, seg, preferred_element_type=F32) + jnp.dot(lo, seg, preferred_element_type=F32))
        return t * lax.rsqrt(ms + NORM_EPS) * gain

    def rope(t):
        partner = jnp.where(first_half, pltpu.roll(t, LANES - ROPE_PAIRS, 1), pltpu.roll(t, ROPE_PAIRS, 1))
        return t * cos + partner * sin

    for j in range(ATTN_WIDTH // LANES):
        sl = slice(j * LANES, (j + 1) * LANES)
        q_ref[:, sl] = rope(head_norm(proj[:, sl], qg_ref[...])) * SCALE
    k = head_norm(proj[:, ATTN_WIDTH:ATTN_WIDTH + KV_WIDTH], kg_ref[...])
    kn_ref[...] = k
    kr_ref[...] = rope(k)
    v_ref[...] = proj[:, ATTN_WIDTH + KV_WIDTH:ATTN_WIDTH + 2 * KV_WIDTH]
    u_ref[...] = proj[:, ATTN_WIDTH + 2 * KV_WIDTH:]


def _in_projection(layer, x_pair, mod3, w_in, q_gain, k_gain, rope_cos, rope_sin):
    tm = TM_PROJ
    stacked = x_pair[0] is x_pair[1]
    p_tiles = P_ROWS // tm
    t_tiles = DEC_SEQ // tm

    def rope_idx(i):
        return (jnp.where(i < p_tiles, 0, 1 + (i - p_tiles) % t_tiles), 0)

    row = lambda w: pl.BlockSpec((tm, w), lambda i: (i, 0))
    full = lambda a: pl.BlockSpec(a.shape, lambda i: (0,) * a.ndim)
    return pl.pallas_call(
        _inproj_kernel,
        grid=(ROWS // tm,),
        in_specs=_group_specs(tm, D_MODEL, stacked) + [
                  _mod_spec(layer, 0, tm), _mod_spec(layer, 1, tm), _layer_spec(w_in, layer),
                  full(q_gain), full(k_gain),
                  pl.BlockSpec((tm, LANES), rope_idx), pl.BlockSpec((tm, LANES), rope_idx)],
        out_specs=[row(ATTN_WIDTH), row(KV_WIDTH), row(KV_WIDTH), row(KV_WIDTH), row(SSM_WIDTH)],
        out_shape=[jax.ShapeDtypeStruct((ROWS, ATTN_WIDTH), F32),
                   jax.ShapeDtypeStruct((ROWS, KV_WIDTH), F32),
                   jax.ShapeDtypeStruct((ROWS, KV_WIDTH), F32),
                   jax.ShapeDtypeStruct((ROWS, KV_WIDTH), F32),
                   jax.ShapeDtypeStruct((ROWS, SSM_WIDTH), F32)],
        scratch_shapes=[pltpu.VMEM((D_MODEL, IN_WIDTH), BF16)],
        compiler_params=_params(1, 48),
        name="in_projection",
    )(*x_pair, mod3, mod3, w_in, q_gain, k_gain, rope_cos, rope_sin)


def _attn_kernel(*refs, tq, has_ctx):
    if has_ctx:
        q_ref, k_ref, v_ref, kc_ref, vc_ref, o_ref = refs
    else:
        q_ref, k_ref, v_ref, o_ref = refs
    nt = (((1,), (1,)), ((), ()))
    for kv in range(N_KV_HEADS):
        sl = slice(kv * HEAD_DIM, (kv + 1) * HEAD_DIM)
        k = k_ref[:, sl].astype(BF16)
        v = v_ref[:, sl].astype(BF16)
        heads = [q_ref[:, (kv * GRP + g) * HEAD_DIM:(kv * GRP + g + 1) * HEAD_DIM] for g in range(GRP)]
        qs = jnp.concatenate(heads, axis=0).astype(BF16)
        s = lax.dot_general(qs, k, nt, preferred_element_type=F32)
        m = jnp.max(s, axis=-1, keepdims=True)
        if has_ctx:
            s_ctx = lax.dot_general(qs, kc_ref[:, sl].astype(BF16), nt, preferred_element_type=F32)
            m = jnp.maximum(m, jnp.max(s_ctx, axis=-1, keepdims=True))
        p = jnp.exp(s - m)
        den = jnp.sum(p, axis=-1, keepdims=True)
        o = jnp.dot(p.astype(BF16), v, preferred_element_type=F32)
        if has_ctx:
            p_ctx = jnp.exp(s_ctx - m)
            den = den + jnp.sum(p_ctx, axis=-1, keepdims=True)
            o = o + jnp.dot(p_ctx.astype(BF16), vc_ref[:, sl].astype(BF16), preferred_element_type=F32)
        o = o / den
        for g in range(GRP):
            h = kv * GRP + g
            o_ref[:, h * HEAD_DIM:(h + 1) * HEAD_DIM] = o[g * tq:(g + 1) * tq]


def _attention(layer, q, k_rot, v, cache_k4, cache_v4):
    tq = TQ_PROMPT
    blk = lambda w: pl.BlockSpec((tq, w), lambda b: (b, 0))
    attn_p = pl.pallas_call(
        functools.partial(_attn_kernel, tq=tq, has_ctx=False),
        grid=(BATCH,),
        in_specs=[blk(ATTN_WIDTH), blk(KV_WIDTH), blk(KV_WIDTH)],
        out_specs=blk(ATTN_WIDTH),
        out_shape=jax.ShapeDtypeStruct((P_ROWS, ATTN_WIDTH), F32),
        compiler_params=_params(1, 32),
        name="attn_prompt",
    )(q, k_rot, v)

    tq = TQ_SAMPLE
    q_tiles = DEC_SEQ // tq
    own = pl.BlockSpec((DEC_SEQ, KV_WIDTH), lambda b, j: (P_ROWS // DEC_SEQ + b, 0))
    ctx = pl.BlockSpec((None, None, PAST_LEN, KV_WIDTH), lambda b, j: (b, layer, 0, 0))
    attn_s = pl.pallas_call(
        functools.partial(_attn_kernel, tq=tq, has_ctx=True),
        grid=(DEC_BATCH, q_tiles),
        in_specs=[pl.BlockSpec((tq, ATTN_WIDTH), lambda b, j: (P_ROWS // tq + b * q_tiles + j, 0)),
                  own, own, ctx, ctx],
        out_specs=pl.BlockSpec((tq, ATTN_WIDTH), lambda b, j: (b * q_tiles + j, 0)),
        out_shape=jax.ShapeDtypeStruct((S_ROWS, ATTN_WIDTH), F32),
        compiler_params=_params(2, 48),
        name="attn_sample",
    )(q, k_rot, v, cache_k4, cache_v4)
    return attn_p, attn_s


def _scan_item(i):
    j = i - BATCH
    seq = jnp.where(i < BATCH, i, BATCH + j // S_CHUNKS)
    bwd = jnp.where(i < BATCH, i, BATCH + (j // S_CHUNKS) * S_CHUNKS + (S_CHUNKS - 1) - j % S_CHUNKS)
    return seq, i, bwd


def _s5_stages(uf_ref, ub_ref, lam_ref, b_ref, c_ref, yf_ref, yb_ref, h, drive_in, drive_out, slab_in, slab_out):
    tc, pitch = SCAN_CHUNK, SCAN_PITCH
    per_dot = tc // (2 * N_GP)

    def section(d, part):
        return (2 * d + part) * N_GP * pitch

    lam = [[lam_ref[d, part] for part in range(2)] for d in range(2)]
    u_bf16 = {}
    y_acc = {}
    h = list(h)
    for k in range(2 * N_GP):
        d, gp = divmod(k, N_GP)
        ch = gp // GP_PER_CHUNK
        u_ref, y_ref = (uf_ref, yf_ref) if d == 0 else (ub_ref, yb_ref)
        if (d, ch) not in u_bf16:
            u_bf16[d, ch] = u_ref[:, ch * LANES:(ch + 1) * LANES].astype(BF16)
        bu = jnp.dot(u_bf16[d, ch], b_ref[d, gp], preferred_element_type=F32)
        drive_out[pl.ds(section(d, 0) + gp * pitch, tc), :] = bu[:, :LANES]
        drive_out[pl.ds(section(d, 1) + gp * pitch, tc), :] = bu[:, LANES:]
        for t in range(k * per_dot, (k + 1) * per_dot):
            for dd in range(2):
                tt = t if dd == 0 else tc - 1 - t
                rows_re = pl.ds(section(dd, 0) + tt, N_GP, stride=pitch)
                rows_im = pl.ds(section(dd, 1) + tt, N_GP, stride=pitch)
                ar, ai = lam[dd]
                hr, hi = h[2 * dd], h[2 * dd + 1]
                nr = ar * hr - ai * hi + drive_in[rows_re, :]
                ni = ar * hi + ai * hr + drive_in[rows_im, :]
                slab_out[rows_re, :] = nr
                slab_out[rows_im, :] = ni
                h[2 * dd], h[2 * dd + 1] = nr, ni
        hs = jnp.concatenate([slab_in[pl.ds(section(d, 0) + gp * pitch, tc), :],
                              slab_in[pl.ds(section(d, 1) + gp * pitch, tc), :]], axis=1).astype(BF16)
        part = jnp.dot(hs, c_ref[d, gp], preferred_element_type=F32)
        y_acc[d, ch] = part if (d, ch) not in y_acc else y_acc[d, ch] + part
        if gp % GP_PER_CHUNK == GP_PER_CHUNK - 1:
            y_ref[:, ch * LANES:(ch + 1) * LANES] = y_acc[d, ch]
    return h


def _s5_kernel(uf_ref, ub_ref, h0_ref, lam_ref, b_ref, c_ref, yf_ref, yb_ref, fin_ref,
               drive0, drive1, slab0, slab1, carry):
    s = pl.program_id(0)

    @pl.when(s == 0)
    def _():
        for buf in (drive0, drive1, slab0, slab1, carry):
            buf[...] = jnp.zeros_like(buf)

    item = s - 1
    valid = (item >= 0) & (item < N_SCAN_ITEMS)
    first = valid & ((item < BATCH) | ((item - BATCH) % S_CHUNKS == 0))
    h = [jnp.where(first, h0_ref[0, k], carry[k]) for k in range(4)]

    def run(drive_in, drive_out, slab_in, slab_out):
        new = _s5_stages(uf_ref, ub_ref, lam_ref, b_ref, c_ref, yf_ref, yb_ref, h,
                         drive_in, drive_out, slab_in, slab_out)
        for k in range(4):
            kept = jnp.where(valid, new[k], h[k])
            carry[k] = kept
            fin_ref[0, k] = kept

    @pl.when(s % 2 == 0)
    def _():
        run(drive1, drive0, slab0, slab1)

    @pl.when(s % 2 == 1)
    def _():
        run(drive0, drive1, slab1, slab0)


def _s5_scan(layer, u, h0, lam_bar, b_pad, c_pad):
    tc = SCAN_CHUNK
    n_seq = BATCH + DEC_BATCH
    last = N_SCAN_ITEMS - 1
    drive_item = lambda s: jnp.minimum(s, last)
    scan_item = lambda s: jnp.clip(s - 1, 0, last)
    read_item = lambda s: jnp.clip(s - 2, 0, last)
    rows = lambda item, which: pl.BlockSpec((tc, SSM_WIDTH), lambda s: (_scan_item(item(s))[which], 0))
    state = pl.BlockSpec((1, 4, N_GP, LANES), lambda s: (_scan_item(scan_item(s))[0], 0, 0, 0))
    state_in = pl.BlockSpec((1, None, 4, N_GP, LANES), lambda s: (_scan_item(scan_item(s))[0], layer, 0, 0, 0))
    slab = pltpu.VMEM((4 * N_GP * SCAN_PITCH, LANES), F32)
    return pl.pallas_call(
        _s5_kernel,
        grid=(N_SCAN_ITEMS + 2,),
        in_specs=[rows(drive_item, 1), rows(drive_item, 2), state_in, _layer_spec(lam_bar, layer),
                  _layer_spec(b_pad, layer), _layer_spec(c_pad, layer)],
        out_specs=[rows(read_item, 1), rows(read_item, 2), state],
        out_shape=[jax.ShapeDtypeStruct((ROWS, SSM_WIDTH), F32),
                   jax.ShapeDtypeStruct((ROWS, SSM_WIDTH), F32),
                   jax.ShapeDtypeStruct((n_seq, 4, N_GP, LANES), F32)],
        scratch_shapes=[slab, slab, slab, slab, pltpu.VMEM((4, N_GP, LANES), F32)],
        compiler_params=_params(1, 56),
        name="s5_scan",
    )(u, u, h0, lam_bar, b_pad, c_pad)


def _s5_discretise(lam_re, lam_im, log_step, b_re, b_im, c_re, c_im):
    delta = jnp.exp(log_step)[..., None]
    mag = jnp.exp(lam_re * delta)
    lbr = mag * jnp.cos(lam_im * delta)
    lbi = mag * jnp.sin(lam_im * delta)
    den = lam_re * lam_re + lam_im * lam_im
    cr = ((lbr - 1.0) * lam_re + lbi * lam_im) / den
    ci = (lbi * lam_re - (lbr - 1.0) * lam_im) / den
    bbr = cr[..., None] * b_re - ci[..., None] * b_im
    bbi = cr[..., None] * b_im + ci[..., None] * b_re
    lam_bar = jnp.stack([lbr, lbi], axis=2).reshape(DEPTH, 2, 2, N_GP, LANES)

    def pair_split(a):
        a = a.reshape(DEPTH, 2, N_GP, 2, *a.shape[3:])
        return a[:, :, :, 0], a[:, :, :, 1]

    slot = jnp.arange(N_GP) % GP_PER_CHUNK

    def place(a, axis):
        parts = [jnp.where((slot == s)[None, None, :, None, None], a, 0.0) for s in range(GP_PER_CHUNK)]
        return jnp.concatenate(parts, axis=axis)

    cat = jnp.concatenate
    re_e, re_o = pair_split(jnp.swapaxes(bbr, -1, -2))
    im_e, im_o = pair_split(jnp.swapaxes(bbi, -1, -2))
    z = jnp.zeros_like(re_e)
    b_small = cat([cat([re_e, z, im_e, z], -1), cat([z, re_o, z, im_o], -1)], -2)
    b_pad = place(b_small, -2)
    cre_e, cre_o = pair_split(jnp.swapaxes(c_re, -1, -2))
    cim_e, cim_o = pair_split(-jnp.swapaxes(c_im, -1, -2))
    z = jnp.zeros_like(cre_e)
    c_small = cat([cat([cre_e, z], -1), cat([z, cre_o], -1), cat([cim_e, z], -1), cat([z, cim_o], -1)], -2)
    c_pad = place(c_small, -1)
    return lam_bar, b_pad.astype(BF16), c_pad.astype(BF16)


def _outproj_kernel(*refs, moe):
    (attn_p_ref, attn_s_ref, yf_ref, yb_ref, u_ref, xp_ref, xs_ref, g1_ref, sh2_ref, sc2_ref, d_ref, wglu_ref,
     bglu_ref, wout_ref, lng_ref, lnb_ref) = refs[:16]
    if moe:
        router_ref, x1_ref, h2_ref, meta_ref, meta_t_ref, cnt_ref, wglu_b, wout_b, cnt = refs[16:]
    else:
        x1_ref, h2_ref, wglu_b, wout_b = refs[16:]
    attn = _group_rows(attn_p_ref, attn_s_ref)
    x = _group_rows(xp_ref, xs_ref)

    @pl.when(pl.program_id(0) == 0)
    def _():
        wglu_b[...] = wglu_ref[...].astype(BF16)
        wout_b[...] = wout_ref[...].astype(BF16)

    y = yf_ref[...] + yb_ref[...] + d_ref[...] * u_ref[...]
    g = 0.5 * y * (1.0 + jnp.tanh(math.sqrt(2.0 / math.pi) * (y + 0.044715 * (y * y * y))))
    z = jnp.dot(g.astype(BF16), wglu_b[...], preferred_element_type=F32) + bglu_ref[...]
    y_ssm = g * _sigmoid(z)
    mix = (jnp.dot(attn.astype(BF16), wout_b[:ATTN_WIDTH, :], preferred_element_type=F32)
           + jnp.dot(y_ssm.astype(BF16), wout_b[ATTN_WIDTH:, :], preferred_element_type=F32))
    x1 = _layer_norm(DEEPNORM_ALPHA * x + g1_ref[0] * mix) * lng_ref[...] + lnb_ref[...]
    x1_ref[...] = x1
    h2 = _layer_norm(x1) * (1.0 + sc2_ref[0]) + sh2_ref[0]
    h2_ref[...] = h2

    if moe:
        @pl.when(pl.program_id(0) == 0)
        def _():
            cnt[...] = jnp.zeros_like(cnt)

        router = router_ref[...]
        h_hi, r_hi = h2.astype(BF16), router.astype(BF16)
        h_lo = (h2 - h_hi.astype(F32)).astype(BF16)
        r_lo = (router - r_hi.astype(F32)).astype(BF16)
        logits = (jnp.dot(h_hi, r_hi, preferred_element_type=F32)
                  + (jnp.dot(h_hi, r_lo, preferred_element_type=F32)
                     + jnp.dot(h_lo, r_hi, preferred_element_type=F32)))
        tm = logits.shape[0]
        lane = lax.broadcasted_iota(jnp.int32, logits.shape, 1).astype(F32)
        neg = jnp.float32(-jnp.inf)
        l1 = jnp.where(lane < N_EXPERTS, logits, neg)
        m1 = jnp.max(l1, axis=-1, keepdims=True)
        i1 = jnp.min(jnp.where(l1 == m1, lane, float(LANES)), axis=-1, keepdims=True)
        l2 = jnp.where(lane == i1, neg, l1)
        m2 = jnp.max(l2, axis=-1, keepdims=True)
        i2 = jnp.min(jnp.where(l2 == m2, lane, float(LANES)), axis=-1, keepdims=True)
        e2 = jnp.exp(m2 - m1)
        w1 = 1.0 / (1.0 + e2)
        w2 = e2 / (1.0 + e2)
        hit = jnp.where((lane == i1) | (lane == i2), 1.0, 0.0)
        r = lax.broadcasted_iota(jnp.int32, (tm, tm), 0)
        c = lax.broadcasted_iota(jnp.int32, (tm, tm), 1)
        earlier = jnp.where(c < r, 1.0, 0.0).astype(BF16)
        rank = cnt[...] + jnp.dot(earlier, hit.astype(BF16), preferred_element_type=F32)
        r1 = jnp.sum(jnp.where(lane == i1, rank, 0.0), axis=-1, keepdims=True)
        r2 = jnp.sum(jnp.where(lane == i2, rank, 0.0), axis=-1, keepdims=True)
        cnt[...] = cnt[...] + jnp.sum(hit, axis=0, keepdims=True)
        cnt_ref[...] = cnt[...]
        fields = (i1, i2, r1, r2, w1, w2)
        meta = jnp.zeros_like(logits)
        for k, f in enumerate(fields):
            meta = jnp.where(lane == float(k), f, meta)
        meta_ref[...] = meta
        meta_t_ref[...] = meta.T[:meta_t_ref.shape[0], :]


def _out_projection(layer, attn_p, attn_s, yf, yb, u, x_pair, mod3, d_skip, w_glu, b_glu, w_out, ln_g, ln_b,
                    router):
    tm = TM_PROJ
    moe = router is not None
    row = lambda w: pl.BlockSpec((tm, w), lambda i: (i, 0))
    full = lambda a: pl.BlockSpec(a.shape, lambda i: (0,) * a.ndim)
    args = [attn_p, attn_s, yf, yb, u, *x_pair, mod3, mod3, mod3, d_skip, w_glu, b_glu, w_out, ln_g, ln_b]
    in_specs = (_group_specs(tm, ATTN_WIDTH, False)
                + [row(SSM_WIDTH), row(SSM_WIDTH), row(SSM_WIDTH)]
                + _group_specs(tm, D_MODEL, x_pair[0] is x_pair[1])
                + [_mod_spec(layer, 2, tm), _mod_spec(layer, 3, tm), _mod_spec(layer, 4, tm),
                   full(d_skip), _layer_spec(w_glu, layer), full(b_glu), _layer_spec(w_out, layer),
                   full(ln_g), full(ln_b)])
    out_specs = [row(D_MODEL), row(D_MODEL)]
    out_shape = [jax.ShapeDtypeStruct((ROWS, D_MODEL), F32), jax.ShapeDtypeStruct((ROWS, D_MODEL), F32)]
    scratch = [pltpu.VMEM((SSM_WIDTH, SSM_WIDTH), BF16), pltpu.VMEM((D_MODEL, D_MODEL), BF16)]
    if moe:
        args.append(router)
        in_specs.append(_layer_spec(router, layer // 2))
        out_specs += [row(LANES), pl.BlockSpec((META_ROWS, tm), lambda i: (0, i)),
                      pl.BlockSpec((1, LANES), lambda i: (0, 0))]
        out_shape += [jax.ShapeDtypeStruct((ROWS, LANES), F32), jax.ShapeDtypeStruct((META_ROWS, ROWS), F32),
                      jax.ShapeDtypeStruct((1, LANES), F32)]
        scratch.append(pltpu.VMEM((1, LANES), F32))
    return pl.pallas_call(
        functools.partial(_outproj_kernel, moe=moe),
        grid=(ROWS // tm,),
        in_specs=in_specs,
        out_specs=out_specs,
        out_shape=out_shape,
        scratch_shapes=scratch,
        compiler_params=_params(1, 48),
        name="out_projection",
    )(*args)


def _swiglu_partial(xb, wg, wu, wd):
    a = jnp.dot(xb, wg, preferred_element_type=F32)
    b = jnp.dot(xb, wu, preferred_element_type=F32)
    act = a * _sigmoid(a) * b
    return act, lambda act_: jnp.dot(act_.astype(BF16), wd, preferred_element_type=F32)


def _ffn_kernel(h_ref, x1_ref, g2_ref, wg_ref, wu_ref, wd_ref, lng_ref, lnb_ref, o_ref, hb, acc, *, tf, d_ff):
    j = pl.program_id(1)

    @pl.when(j == 0)
    def _():
        hb[...] = h_ref[...].astype(BF16)
        acc[...] = jnp.zeros_like(acc)

    last = pl.num_programs(1) - 1
    tail = d_ff % tf

    def block(width, valid):
        wd = wd_ref[:width, :]
        if valid < width:
            r = lax.broadcasted_iota(jnp.int32, wd.shape, 0)
            wd = jnp.where(r < valid, wd, 0.0)
        act, down = _swiglu_partial(hb[...], wg_ref[:, :width].astype(BF16), wu_ref[:, :width].astype(BF16),
                                    wd.astype(BF16))
        if valid < width:
            col = lax.broadcasted_iota(jnp.int32, act.shape, 1)
            act = jnp.where(col < valid, act, 0.0)
        acc[...] += down(act)

    if tail:
        @pl.when(j < last)
        def _():
            block(tf, tf)

        @pl.when(j == last)
        def _():
            block(-(-tail // LANES) * LANES, tail)
    else:
        block(tf, tf)

    @pl.when(j == last)
    def _():
        y = DEEPNORM_ALPHA * x1_ref[...] + g2_ref[0] * acc[...]
        o_ref[...] = _layer_norm(y) * lng_ref[...] + lnb_ref[...]


def _ffn(layer, h2, x1, mod3, wg, wu, wd, ln_g, ln_b):
    tm, tf, d_ff = TM_FFN, TF_DENSE, D_FF
    w_idx = layer // 2
    up = pl.BlockSpec((None, D_MODEL, tf), lambda i, j: (w_idx, 0, j))
    down = pl.BlockSpec((None, tf, D_MODEL), lambda i, j: (w_idx, j, 0))
    row = lambda w: pl.BlockSpec((tm, w), lambda i, j: (i, 0))
    vec = pl.BlockSpec((1, D_MODEL), lambda i, j: (0, 0))
    g2 = pl.BlockSpec((1, 1, D_MODEL), lambda i, j: (layer * N_MOD + _mod_row(i, tm), 0, 5))
    return pl.pallas_call(
        functools.partial(_ffn_kernel, tf=tf, d_ff=d_ff),
        grid=(ROWS // tm, pl.cdiv(d_ff, tf)),
        in_specs=[row(D_MODEL), row(D_MODEL), g2, up, up, down, vec, vec],
        out_specs=row(D_MODEL),
        out_shape=jax.ShapeDtypeStruct((ROWS, D_MODEL), F32),
        scratch_shapes=[pltpu.VMEM((tm, D_MODEL), BF16), pltpu.VMEM((tm, D_MODEL), F32)],
        compiler_params=_params(2, 48),
        name="ffn_dense",
    )(h2, x1, mod3, wg, wu, wd, ln_g, ln_b)


def _route_plan(meta_t, counts):
    i32 = jnp.int32
    e1, e2 = meta_t[0].astype(i32), meta_t[1].astype(i32)
    r1, r2 = meta_t[2].astype(i32), meta_t[3].astype(i32)
    cnt = counts[0, :N_EXPERTS].astype(i32)
    tiles = (cnt + MOE_TILE - 1) // MOE_TILE
    tile_end = jnp.cumsum(tiles)
    tile_start = tile_end - tiles
    base = tile_start * MOE_TILE
    pos1 = (base[e1] + r1) * ROW_TILE
    pos2 = (base[e2] + r2) * ROW_TILE
    t = jnp.arange(MOE_TILES, dtype=i32)
    n_used = tile_end[-1]
    used = t < n_used
    owner = jnp.sum((t[:, None] >= tile_end[None, :]).astype(i32), axis=1)
    owner = jnp.minimum(owner, N_EXPERTS - 1)
    rows = jnp.clip(cnt[owner] - (t - tile_start[owner]) * MOE_TILE, 0, MOE_TILE)
    rows = jnp.where(used, rows, 0)
    n_sub = (rows + MOE_SUB - 1) // MOE_SUB
    last_owner = owner[jnp.maximum(n_used - 1, 0)]
    tile_expert = jnp.where(used, owner, last_owner)
    zero_fill = (rows < MOE_TILE).astype(i32)
    return pos1, pos2, tile_expert, n_sub, zero_fill


def _slot_rows(start):
    return pl.ds(pl.multiple_of(start, ROW_TILE), ROW_TILE)


def _dispatch_kernel(zf_ref, p1_ref, p2_ref, h_ref, xs_ref, src0, src1, zbuf, zsem, sem0, sem1):
    tm = h_ref.shape[0]
    tile_rows = MOE_TILE * ROW_TILE

    def zero_copy(k):
        return pltpu.make_async_copy(zbuf, xs_ref.at[pl.ds(k * tile_rows, tile_rows)], zsem)

    @pl.when(pl.program_id(0) == 0)
    def _():
        zbuf[...] = jnp.zeros_like(zbuf)
        for k in range(MOE_TILES):
            @pl.when(zf_ref[k] != 0)
            def _():
                zero_copy(k).start()
        for k in range(MOE_TILES):
            @pl.when(zf_ref[k] != 0)
            def _():
                zero_copy(k).wait()

    step = pl.program_id(0)

    def row_copy(src, sem, r, slot):
        return pltpu.make_async_copy(src.at[_slot_rows(r * ROW_TILE)], xs_ref.at[_slot_rows(slot)], sem)

    def drain(src, sem):
        def wait(r, carry):
            row_copy(src, sem, 0, 0).wait()
            row_copy(src, sem, 0, 0).wait()
            return carry
        lax.fori_loop(0, tm, wait, 0, unroll=DMA_UNROLL)

    def scatter(src, sem, src_prev, sem_prev):
        for s in range(ROW_TILE):
            src[pl.ds(s, tm, stride=ROW_TILE), :] = h_ref[:, s * LANES:(s + 1) * LANES]

        def start(r, carry):
            row_copy(src, sem, r, p1_ref[r]).start()
            row_copy(src, sem, r, p2_ref[r]).start()
            return carry

        lax.fori_loop(0, tm, start, 0, unroll=DMA_UNROLL)

        @pl.when(step > 0)
        def _():
            drain(src_prev, sem_prev)

        @pl.when(step == pl.num_programs(0) - 1)
        def _():
            drain(src, sem)

    @pl.when(step % 2 == 0)
    def _():
        scatter(src0, sem0, src1, sem1)

    @pl.when(step % 2 == 1)
    def _():
        scatter(src1, sem1, src0, sem0)


def _moe_dispatch(h2, pos1, pos2, zero_fill):
    tm = TM_ROUTE
    pos = pl.BlockSpec((tm,), lambda i, zf: (i,), memory_space=pltpu.SMEM)
    return pl.pallas_call(
        _dispatch_kernel,
        grid_spec=pltpu.PrefetchScalarGridSpec(
            num_scalar_prefetch=1,
            grid=(ROWS // tm,),
            in_specs=[pos, pos, pl.BlockSpec((tm, D_MODEL), lambda i, zf: (i, 0))],
            out_specs=pl.BlockSpec(memory_space=pl.ANY),
            scratch_shapes=[pltpu.VMEM((tm * ROW_TILE, LANES), F32), pltpu.VMEM((tm * ROW_TILE, LANES), F32),
                            pltpu.VMEM((MOE_TILE * ROW_TILE, LANES), F32),
                            pltpu.SemaphoreType.DMA(()), pltpu.SemaphoreType.DMA(()),
                            pltpu.SemaphoreType.DMA(())]),
        out_shape=jax.ShapeDtypeStruct((MOE_ROWS * ROW_TILE, LANES), F32),
        compiler_params=_params(1, 32),
        name="moe_dispatch",
    )(zero_fill, pos1, pos2, h2)


def _expert_kernel(te_ref, ns_ref, xs_ref, wg_ref, wu_ref, wd_ref, ys_ref, xb, acc):
    i, j = pl.program_id(0), pl.program_id(1)
    n_sub = ns_ref[i]

    @pl.when(j == 0)
    def _():
        acc[...] = jnp.zeros_like(acc)

    @pl.when((j == 0) & (n_sub > 0))
    def _():
        for s in range(ROW_TILE):
            xb[:, s * LANES:(s + 1) * LANES] = xs_ref[pl.ds(s, MOE_TILE, stride=ROW_TILE), :].astype(BF16)

    def accumulate(rows):
        act, down = _swiglu_partial(xb[rows, :], wg_ref[...].astype(BF16), wu_ref[...].astype(BF16),
                                    wd_ref[...].astype(BF16))
        acc[rows, :] += down(act)

    full = MOE_TILE // MOE_SUB

    @pl.when(n_sub == full)
    def _():
        accumulate(slice(None))

    @pl.when((n_sub > 0) & (n_sub < full))
    def _():
        def body(s, carry):
            accumulate(pl.ds(pl.multiple_of(s * MOE_SUB, MOE_SUB), MOE_SUB))
            return carry
        lax.fori_loop(0, n_sub, body, 0)

    @pl.when(j == pl.num_programs(1) - 1)
    def _():
        for s in range(ROW_TILE):
            ys_ref[pl.ds(s, MOE_TILE, stride=ROW_TILE), :] = acc[:, s * LANES:(s + 1) * LANES]


def _moe_experts(layer, xs, tile_expert, n_sub, wg, wu, wd):
    tf = TF_MOE
    n_j = EXPERT_FF // tf
    w_idx = layer // 2
    ff = lambda i, j, ns: jnp.where(ns[i] > 0, j, n_j - 1)
    up = pl.BlockSpec((None, None, D_MODEL, tf), lambda i, j, te, ns: (w_idx, te[i], 0, ff(i, j, ns)))
    down = pl.BlockSpec((None, None, tf, D_MODEL), lambda i, j, te, ns: (w_idx, te[i], ff(i, j, ns), 0))
    rows = pl.BlockSpec((MOE_TILE * ROW_TILE, LANES), lambda i, j, te, ns: (i, 0))
    return pl.pallas_call(
        _expert_kernel,
        grid_spec=pltpu.PrefetchScalarGridSpec(
            num_scalar_prefetch=2,
            grid=(MOE_TILES, n_j),
            in_specs=[rows, up, up, down],
            out_specs=rows,
            scratch_shapes=[pltpu.VMEM((MOE_TILE, D_MODEL), BF16), pltpu.VMEM((MOE_TILE, D_MODEL), F32)]),
        out_shape=jax.ShapeDtypeStruct((MOE_ROWS * ROW_TILE, LANES), F32),
        compiler_params=_params(2, 48),
        name="moe_experts",
    )(tile_expert, n_sub, xs, wg, wu, wd)


def _combine_kernel(p1_ref, p2_ref, q1_ref, q2_ref, ys_ref, x1_ref, meta_ref, g2_ref, lng_ref, lnb_ref, *rest,
                    split):
    a1, a2, b1, b2, f, sem_a, sem_b = rest[-7:]
    tm = x1_ref.shape[0]
    step = pl.program_id(0)
    meta = meta_ref[...]
    w1, w2 = meta[:, 4:5], meta[:, 5:6]

    def row_copy(slot, buf, r, sem):
        return pltpu.make_async_copy(ys_ref.at[_slot_rows(slot)], buf.at[_slot_rows(r * ROW_TILE)], sem)

    def request(s1_ref, s2_ref, buf1, buf2, sem):
        def start(r, carry):
            row_copy(s1_ref[r], buf1, r, sem).start()
            row_copy(s2_ref[r], buf2, r, sem).start()
            return carry
        lax.fori_loop(0, tm, start, 0, unroll=DMA_UNROLL)

    def gather(buf1, buf2, sem, nxt1, nxt2, sem_nxt):
        @pl.when(step == 0)
        def _():
            request(p1_ref, p2_ref, buf1, buf2, sem)

        @pl.when(step < pl.num_programs(0) - 1)
        def _():
            request(q1_ref, q2_ref, nxt1, nxt2, sem_nxt)

        def wait(r, carry):
            row_copy(0, buf1, 0, sem).wait()
            row_copy(0, buf2, 0, sem).wait()
            return carry
        lax.fori_loop(0, tm, wait, 0, unroll=DMA_UNROLL)
        for s in range(ROW_TILE):
            rows = pl.ds(s, tm, stride=ROW_TILE)
            f[:, s * LANES:(s + 1) * LANES] = w1 * buf1[rows, :] + w2 * buf2[rows, :]

    @pl.when(step % 2 == 0)
    def _():
        gather(a1, a2, sem_a, b1, b2, sem_b)

    @pl.when(step % 2 == 1)
    def _():
        gather(b1, b2, sem_b, a1, a2, sem_a)

    y = DEEPNORM_ALPHA * x1_ref[...] + g2_ref[0] * f[...]
    out = _layer_norm(y) * lng_ref[...] + lnb_ref[...]
    if split:
        op_ref, os_ref = rest[:2]
        is_prompt = pl.program_id(0) < P_ROWS // tm

        @pl.when(is_prompt)
        def _():
            op_ref[...] = out

        @pl.when(jnp.logical_not(is_prompt))
        def _():
            os_ref[...] = out
    else:
        rest[0][...] = out


def _moe_combine(layer, ys, pos1, pos2, x1, meta, mod3, ln_g, ln_b, split):
    tm = TM_COMBINE
    n_steps = ROWS // tm
    pos = pl.BlockSpec((tm,), lambda i: (i,), memory_space=pltpu.SMEM)
    pos_next = pl.BlockSpec((tm,), lambda i: (jnp.minimum(i + 1, n_steps - 1),), memory_space=pltpu.SMEM)
    row = lambda w: pl.BlockSpec((tm, w), lambda i: (i, 0))
    vec = pl.BlockSpec((1, D_MODEL), lambda i: (0, 0))
    slots = pltpu.VMEM((tm * ROW_TILE, LANES), F32)
    if split:
        out_specs = _group_specs(tm, D_MODEL, False)
        out_shape = [jax.ShapeDtypeStruct((P_ROWS, D_MODEL), F32), jax.ShapeDtypeStruct((S_ROWS, D_MODEL), F32)]
    else:
        out_specs = row(D_MODEL)
        out_shape = jax.ShapeDtypeStruct((ROWS, D_MODEL), F32)
    return pl.pallas_call(
        functools.partial(_combine_kernel, split=split),
        grid=(n_steps,),
        in_specs=[pos, pos, pos_next, pos_next, pl.BlockSpec(memory_space=pl.ANY), row(D_MODEL), row(LANES),
                  _mod_spec(layer, 5, tm), vec, vec],
        out_specs=out_specs,
        out_shape=out_shape,
        scratch_shapes=[slots, slots, slots, slots, pltpu.VMEM((tm, D_MODEL), F32),
                        pltpu.SemaphoreType.DMA(()), pltpu.SemaphoreType.DMA(())],
        compiler_params=_params(1, 32),
        name="moe_combine",
    )(pos1, pos2, pos1, pos2, ys, x1, meta, mod3, ln_g, ln_b)


def _rope_tables():
    rows = DEC_SEQ // GRID_W
    row = jnp.repeat(jnp.arange(rows), GRID_W).astype(F32)
    col = jnp.tile(jnp.arange(GRID_W), rows).astype(F32)
    freqs = ROPE_THETA ** (-jnp.arange(ROPE_PAIRS, dtype=F32) / ROPE_PAIRS)
    ang = jnp.stack([row[:, None] * freqs, col[:, None] * freqs], axis=1)
    cos, sin = jnp.cos(ang), jnp.sin(ang)
    cos_h = jnp.concatenate([cos[:, 0], cos[:, 0], cos[:, 1], cos[:, 1]], axis=-1)
    sin_h = jnp.concatenate([-sin[:, 0], sin[:, 0], -sin[:, 1], sin[:, 1]], axis=-1)
    per_tile = LANES // HEAD_DIM
    cos_t = jnp.concatenate([jnp.ones((ROPE_ID_ROWS, LANES), F32), jnp.tile(cos_h, (1, per_tile))], axis=0)
    sin_t = jnp.concatenate([jnp.zeros((ROPE_ID_ROWS, LANES), F32), jnp.tile(sin_h, (1, per_tile))], axis=0)
    return cos_t, sin_t


def kernel(x_prompt, x_sample, c, cache_k, cache_v, state_ssm_re, state_ssm_im, c_ctx, w_ada, b_ada,
           w_in, q_gain, k_gain, ssm_lambda_re, ssm_lambda_im, ssm_log_step, ssm_b_re, ssm_b_im,
           ssm_c_re, ssm_c_im, ssm_d, w_glu, b_glu, w_out, ln1_g, ln1_b, ln2_g, ln2_b,
           ffn_w_gate, ffn_w_up, ffn_w_down, router_w, moe_w_gate, moe_w_up, moe_w_down):
    x_pair = (x_prompt.reshape(P_ROWS, D_MODEL), x_sample.reshape(S_ROWS, D_MODEL))
    cvec = jnp.concatenate([c_ctx[None], c, jnp.zeros((N_MOD - 1 - DEC_BATCH, D_MODEL), F32)], axis=0)
    mod3 = _modulation(cvec, w_ada, b_ada).reshape(DEPTH * N_MOD, 1, 6 * D_MODEL)
    rope_cos, rope_sin = _rope_tables()
    lam_bar, b_pad, c_pad = _s5_discretise(ssm_lambda_re, ssm_lambda_im, ssm_log_step,
                                           ssm_b_re, ssm_b_im, ssm_c_re, ssm_c_im)
    cache_k4 = cache_k.reshape(DEC_BATCH, DEPTH, PAST_LEN, KV_WIDTH)
    cache_v4 = cache_v.reshape(DEC_BATCH, DEPTH, PAST_LEN, KV_WIDTH)
    h0_s = jnp.stack([state_ssm_re, state_ssm_im], axis=3)
    h0_s = h0_s.reshape(DEC_BATCH, DEPTH, 4, N_GP, LANES)
    h0_all = jnp.concatenate([jnp.zeros((BATCH, DEPTH, 4, N_GP, LANES), F32), h0_s], axis=0)
    router_pad = jnp.pad(router_w, ((0, 0), (0, 0), (0, LANES - N_EXPERTS)))
    gain2 = lambda g: jnp.tile(g, (1, LANES // HEAD_DIM))

    new_k, new_v, new_re, new_im = [], [], [], []
    for l in range(DEPTH):
        q, k_rot, k_norm, v, u = _in_projection(l, x_pair, mod3, w_in, gain2(q_gain[l:l + 1]),
                                                gain2(k_gain[l:l + 1]), rope_cos, rope_sin)
        attn_p, attn_s = _attention(l, q, k_rot, v, cache_k4, cache_v4)
        yf, yb, fin = _s5_scan(l, u, h0_all, lam_bar, b_pad, c_pad)
        moe = l % 2 == 1
        outs = _out_projection(l, attn_p, attn_s, yf, yb, u, x_pair, mod3, ssm_d[l:l + 1], w_glu,
                               b_glu[l:l + 1], w_out, ln1_g[l:l + 1], ln1_b[l:l + 1],
                               router_pad if moe else None)
        if moe:
            x1, h2, meta, meta_t, counts = outs
            pos1, pos2, tile_expert, n_sub, zero_fill = _route_plan(meta_t, counts)
            xs = _moe_dispatch(h2, pos1, pos2, zero_fill)
            ys = _moe_experts(l, xs, tile_expert, n_sub, moe_w_gate, moe_w_up, moe_w_down)
            last = l == DEPTH - 1
            x = _moe_combine(l, ys, pos1, pos2, x1, meta, mod3, ln2_g[l:l + 1], ln2_b[l:l + 1], split=last)
            x_pair = tuple(x) if last else (x, x)
        else:
            x1, h2 = outs
            x = _ffn(l, h2, x1, mod3, ffn_w_gate, ffn_w_up, ffn_w_down, ln2_g[l:l + 1], ln2_b[l:l + 1])
            x_pair = (x, x)
        new_k.append(k_norm[:P_ROWS].reshape(BATCH, SEQ, N_KV_HEADS, HEAD_DIM))
        new_v.append(v[:P_ROWS].reshape(BATCH, SEQ, N_KV_HEADS, HEAD_DIM))
        fin_p = fin[:BATCH].reshape(BATCH, 2, 2, SSM_GROUPS, SSM_STATE)
        new_re.append(fin_p[:, :, 0])
        new_im.append(fin_p[:, :, 1])

    assert DEPTH % 2 == 0, "the last layer is the routed one and returns the two groups separately"
    y_prompt = x_pair[0].reshape(BATCH, SEQ, D_MODEL)
    y_sample = x_pair[1].reshape(DEC_BATCH, DEC_SEQ, D_MODEL)
    return (y_prompt, y_sample, jnp.stack(new_k, axis=1), jnp.stack(new_v, axis=1),
            jnp.stack(new_re, axis=1), jnp.stack(new_im, axis=1))
```
